```python
import math
import jax, jax.numpy as jnp
from jax import lax
import numpy as np

D_MODEL = 2048
BATCH = 4
SEQ = 4096
DEPTH = 1

D_MIX = D_MODEL
GLA_V = D_MIX // 2
CONV_CH = D_MIX - GLA_V
GLA_HEADS = 4
GLA_QK = GLA_V // 2
GLA_DK = GLA_QK // GLA_HEADS
GLA_DV = GLA_V // GLA_HEADS
GATE_RANK = 16
GATE_TAU = 16.0
GLA_CHUNK = 64
CONV_WIDTH = 31
D_FF = 5632
PLE_DIM = 256
LN_EPS = 1e-5
ALPHA = (2.0 * DEPTH) ** 0.25
BETA = (8.0 * DEPTH) ** -0.25
MIX_IN_COLS = 2 * GLA_QK + 2 * GLA_V + GATE_RANK + 2 * CONV_CH
MIX_SPLITS = (GLA_QK, 2 * GLA_QK, 2 * GLA_QK + GLA_V, 2 * GLA_QK + 2 * GLA_V,
              2 * GLA_QK + 2 * GLA_V + GATE_RANK)

kernel_name = 'hybrid_gla_conformer_macaron_deepnorm_ple'


def _layer_norm(x, g, b):
    xf = x.astype(jnp.float32)
    mu = jnp.mean(xf, axis=-1, keepdims=True)
    var = jnp.mean(jnp.square(xf - mu), axis=-1, keepdims=True)
    return ((xf - mu) * lax.rsqrt(var + LN_EPS) * g + b).astype(x.dtype)


def _swiglu_ffn(x, w_in, w_out):
    gate, up = jnp.split(x @ w_in, 2, axis=-1)
    return (jax.nn.silu(gate) * up) @ w_out


def _gla_chunked(q, k, v, log_a):
    B, S, H, DK = q.shape
    DV = v.shape[-1]
    n_chunks = S // GLA_CHUNK

    def to_chunks(t):
        return t.reshape(B, n_chunks, GLA_CHUNK, H, t.shape[-1]).transpose(1, 0, 3, 2, 4).astype(jnp.float32)

    qc, kc, vc, gc = to_chunks(q), to_chunks(k), to_chunks(v), to_chunks(log_a)
    causal = jnp.tril(jnp.ones((GLA_CHUNK, GLA_CHUNK), dtype=bool))[:, :, None]

    def step(state, inp):
        qb, kb, vb, gb = inp
        cum = jnp.cumsum(gb, axis=2)
        o_inter = jnp.einsum('bhtk,bhkv->bhtv', qb * jnp.exp(cum), state)
        rel = cum[:, :, :, None, :] - cum[:, :, None, :, :]
        decay = jnp.exp(jnp.where(causal, rel, -jnp.inf))
        scores = jnp.einsum('bhtk,bhsk,bhtsk->bhts', qb, kb, decay)
        o = o_inter + jnp.einsum('bhts,bhsv->bhtv', scores, vb)
        last = cum[:, :, -1:, :]
        k_dec = kb * jnp.exp(last - cum)
        state = jnp.exp(last[:, :, 0, :, None]) * state + jnp.einsum('bhsk,bhsv->bhkv', k_dec, vb)
        return state, o

    s0 = jnp.zeros((B, H, DK, DV), jnp.float32)
    _, out = lax.scan(step, s0, (qc, kc, vc, gc))
    return out.transpose(1, 0, 3, 2, 4).reshape(B, S, H, DV)


def _hybrid_mixer(h, w_mix_in, gla_w_gate, gla_b_gate, gla_norm_g,
                  conv_w, conv_b, conv_ln_g, conv_ln_b, w_mix_out):
    B, S, _ = h.shape
    proj = h @ w_mix_in
    q, k, v, g_out, a_lr, glu_in = jnp.split(proj, MIX_SPLITS, axis=-1)

    q = q.reshape(B, S, GLA_HEADS, GLA_DK) * (GLA_DK ** -0.5)
    k = k.reshape(B, S, GLA_HEADS, GLA_DK)
    v = v.reshape(B, S, GLA_HEADS, GLA_DV)
    z = (a_lr @ gla_w_gate + gla_b_gate).astype(jnp.float32)
    log_a = (jax.nn.log_sigmoid(z) / GATE_TAU).reshape(B, S, GLA_HEADS, GLA_DK)
    o = _gla_chunked(q, k, v, log_a)
    o = o * lax.rsqrt(jnp.mean(jnp.square(o), axis=-1, keepdims=True) + LN_EPS) * gla_norm_g
    o = o.astype(h.dtype) * jax.nn.silu(g_out.reshape(B, S, GLA_HEADS, GLA_DV))
    gla_out = o.reshape(B, S, GLA_V)

    u = jax.nn.glu(glu_in, axis=-1)
    u = lax.conv_general_dilated(
        u, conv_w[:, None, :].astype(u.dtype), window_strides=(1,),
        padding=[(CONV_WIDTH - 1, 0)],
        dimension_numbers=('NWC', 'WIO', 'NWC'),
        feature_group_count=CONV_CH) + conv_b
    u = jax.nn.silu(_layer_norm(u, conv_ln_g, conv_ln_b))

    return jnp.concatenate([gla_out, u], axis=-1) @ w_mix_out


def setup_inputs(seed: int = 0) -> dict:
    key = jax.random.key(seed)
    ks = iter(jax.random.split(key, 40))

    def nrm(shape, scale):
        return jax.random.normal(next(ks), shape, jnp.float32) * scale

    def gain(n):
        return 1.0 + nrm((DEPTH, n), 0.02)

    def bias(n):
        return nrm((DEPTH, n), 0.02)

    s_in = D_MODEL ** -0.5
    w_mix_in = jnp.concatenate([
        nrm((DEPTH, D_MODEL, 2 * GLA_QK), s_in),
        nrm((DEPTH, D_MODEL, GLA_V), s_in * BETA),
        nrm((DEPTH, D_MODEL, GLA_V + GATE_RANK + 2 * CONV_CH), s_in),
    ], axis=-1)
    return {
        'x': nrm((BATCH, SEQ, D_MODEL), 1.0),
        'p': nrm((DEPTH, BATCH, SEQ, PLE_DIM), 1.0),
        'ffn1_w_in': nrm((DEPTH, D_MODEL, 2 * D_FF), s_in),
        'ffn1_w_out': nrm((DEPTH, D_FF, D_MODEL), D_FF ** -0.5 * BETA),
        'ln_ffn1_g': gain(D_MODEL),
        'ln_ffn1_b': bias(D_MODEL),
        'w_mix_in': w_mix_in,
        'gla_w_gate': nrm((DEPTH, GATE_RANK, GLA_QK), GATE_RANK ** -0.5),
        'gla_b_gate': bias(GLA_QK),
        'gla_norm_g': gain(GLA_DV),
        'conv_w': nrm((DEPTH, CONV_WIDTH, CONV_CH), CONV_WIDTH ** -0.5),
        'conv_b': bias(CONV_CH),
        'conv_ln_g': gain(CONV_CH),
        'conv_ln_b': bias(CONV_CH),
        'w_mix_out': nrm((DEPTH, D_MIX, D_MODEL), D_MIX ** -0.5 * BETA),
        'ln_mix_g': gain(D_MODEL),
        'ln_mix_b': bias(D_MODEL),
        'ffn2_w_in': nrm((DEPTH, D_MODEL, 2 * D_FF), s_in),
        'ffn2_w_out': nrm((DEPTH, D_FF, D_MODEL), D_FF ** -0.5 * BETA),
        'ln_ffn2_g': gain(D_MODEL),
        'ln_ffn2_b': bias(D_MODEL),
        'ple_w_gate': nrm((DEPTH, D_MODEL, D_MODEL), s_in),
        'ple_b_gate': bias(D_MODEL),
        'ple_w_proj': nrm((DEPTH, PLE_DIM, D_MODEL), PLE_DIM ** -0.5 * BETA),
        'ln_ple_g': gain(D_MODEL),
        'ln_ple_b': bias(D_MODEL),
    }


def reference(x, p, ffn1_w_in, ffn1_w_out, ln_ffn1_g, ln_ffn1_b, w_mix_in, gla_w_gate,
              gla_b_gate, gla_norm_g, conv_w, conv_b, conv_ln_g, conv_ln_b, w_mix_out,
              ln_mix_g, ln_mix_b, ffn2_w_in, ffn2_w_out, ln_ffn2_g, ln_ffn2_b,
              ple_w_gate, ple_b_gate, ple_w_proj, ln_ple_g, ln_ple_b):
    for i in range(DEPTH):
        x = _layer_norm(ALPHA * x + 0.5 * _swiglu_ffn(x, ffn1_w_in[i], ffn1_w_out[i]),
                        ln_ffn1_g[i], ln_ffn1_b[i])
        mix = _hybrid_mixer(x, w_mix_in[i], gla_w_gate[i], gla_b_gate[i], gla_norm_g[i],
                            conv_w[i], conv_b[i], conv_ln_g[i], conv_ln_b[i], w_mix_out[i])
        x = _layer_norm(ALPHA * x + mix, ln_mix_g[i], ln_mix_b[i])
        x = _layer_norm(ALPHA * x + 0.5 * _swiglu_ffn(x, ffn2_w_in[i], ffn2_w_out[i]),
                        ln_ffn2_g[i], ln_ffn2_b[i])
        gate = jax.nn.sigmoid(x @ ple_w_gate[i] + ple_b_gate[i])
        x = _layer_norm(ALPHA * x + gate * (p[i] @ ple_w_proj[i]), ln_ple_g[i], ln_ple_b[i])
    return x
```

```python
import functools

import jax
import jax.numpy as jnp
from jax import lax
from jax.experimental import pallas as pl
from jax.experimental.pallas import tpu as pltpu

DEPTH = 1
GLA_HEADS = 4
GLA_DK = 128
GLA_DV = 256
GLA_QK = GLA_HEADS * GLA_DK
GLA_V = GLA_HEADS * GLA_DV
CONV_CH = 1024
GATE_RANK = 16
GATE_TAU = 16.0
CONV_WIDTH = 31
LN_EPS = 1e-5
ALPHA = (2.0 * DEPTH) ** 0.25

QKVG_COLS = 2 * GLA_QK + 2 * GLA_V
GLU_START = QKVG_COLS + GATE_RANK

SUBLANES = 8
CONV_HALO = 32

BF16 = jnp.bfloat16
F32 = jnp.float32

VMEM_LIMIT = 56 * 1024 * 1024


def _dot(a, b):
    return jnp.dot(a, b, preferred_element_type=F32)


def _dot_nt(a, b):
    return lax.dot_general(a, b, (((1,), (1,)), ((), ())), preferred_element_type=F32)


def _dot_tn(a, b):
    return lax.dot_general(a, b, (((0,), (0,)), ((), ())), preferred_element_type=F32)


def _layer_norm(y, g, b):
    mu = jnp.mean(y, axis=-1, keepdims=True)
    d = y - mu
    var = jnp.mean(d * d, axis=-1, keepdims=True)
    return d * lax.rsqrt(var + LN_EPS) * g + b


def _silu(x):
    return x * jax.nn.sigmoid(x)


def _params(sem):
    return pltpu.CompilerParams(dimension_semantics=sem, vmem_limit_bytes=VMEM_LIMIT)


def _ffn_kernel(x_ref, wg_ref, wu_ref, wo_ref, g_ref, b_ref, o_ref, xb_ref):
    j = pl.program_id(1)

    @pl.when(j == 0)
    def _():
        xb_ref[...] = x_ref[...].astype(BF16)
        o_ref[...] = jnp.zeros_like(o_ref)

    xb = xb_ref[...]
    gate = _dot(xb, wg_ref[...])
    up = _dot(xb, wu_ref[...])
    act = (_silu(gate) * up).astype(BF16)
    o_ref[...] += _dot(act, wo_ref[...])

    @pl.when(j == pl.num_programs(1) - 1)
    def _():
        y = ALPHA * x_ref[...] + 0.5 * o_ref[...]
        o_ref[...] = _layer_norm(y, g_ref[...], b_ref[...])


def _ffn(x, w_in, w_out, g, b, *, tm, tf):
    m, d = x.shape
    d_ff = w_out.shape[0]
    n_ff = d_ff // tf
    assert m % tm == 0 and d_ff % tf == 0
    return pl.pallas_call(
        _ffn_kernel,
        grid=(m // tm, n_ff),
        in_specs=[
            pl.BlockSpec((tm, d), lambda i, j: (i, 0)),
            pl.BlockSpec((d, tf), lambda i, j: (0, j)),
            pl.BlockSpec((d, tf), lambda i, j: (0, j + n_ff)),
            pl.BlockSpec((tf, d), lambda i, j: (j, 0)),
            pl.BlockSpec((1, d), lambda i, j: (0, 0)),
            pl.BlockSpec((1, d), lambda i, j: (0, 0)),
        ],
        out_specs=pl.BlockSpec((tm, d), lambda i, j: (i, 0)),
        out_shape=jax.ShapeDtypeStruct((m, d), F32),
        scratch_shapes=[pltpu.VMEM((tm, d), BF16)],
        compiler_params=_params(("parallel", "arbitrary")),
        name="ffn",
    )(x, w_in, w_in, w_out, g, b)


def _proj_kernel(x_ref, w_ref, o_ref):
    o_ref[...] = _dot(x_ref[...].astype(BF16), w_ref[...]).astype(o_ref.dtype)


def _proj(x, w, out_dtype, *, tm, tn):
    m, d = x.shape
    n = w.shape[1]
    assert m % tm == 0 and n % tn == 0
    return pl.pallas_call(
        _proj_kernel,
        grid=(m // tm, n // tn),
        in_specs=[
            pl.BlockSpec((tm, d), lambda i, j: (i, 0)),
            pl.BlockSpec((d, tn), lambda i, j: (0, j)),
        ],
        out_specs=pl.BlockSpec((tm, tn), lambda i, j: (i, j)),
        out_shape=jax.ShapeDtypeStruct((m, n), out_dtype),
        compiler_params=_params(("parallel", "arbitrary")),
        name="proj",
    )(x, w)


def _split_bf16(x):
    hi = x.astype(BF16)
    lo = (x - hi.astype(F32)).astype(BF16)
    return hi, lo


def _gla_kernel(qkvg_ref, alr_ref, wgate_ref, bgate_ref, ng_ref, o_ref, state_ref, *, c):
    @pl.when(pl.program_id(1) == 0)
    def _():
        state_ref[...] = jnp.zeros_like(state_ref)

    t = qkvg_ref.shape[1]
    row = lax.broadcasted_iota(jnp.int32, (c, c), 0)
    col = lax.broadcasted_iota(jnp.int32, (c, c), 1)
    causal = row >= col
    tril = causal.astype(BF16)
    scale = GLA_DK ** -0.5

    for blk in range(t // c):
        rows = pl.ds(blk * c, c)
        z = jnp.dot(alr_ref[0, rows, :], wgate_ref[...], preferred_element_type=F32,
                    precision=lax.Precision.HIGHEST) + bgate_ref[...]
        log_a = jax.nn.log_sigmoid(z) / GATE_TAU
        la_hi, la_lo = _split_bf16(log_a)
        cum = _dot(tril, la_hi) + _dot(tril, la_lo)
        c_mid = cum[c // 2 - 1:c // 2, :]
        c_last = cum[c - 1:c, :]
        e_q = jnp.exp(cum - c_mid)
        e_k = jnp.exp(c_mid - cum)
        e_in = jnp.exp(cum)
        e_out = jnp.exp(c_last - cum)
        e_last = jnp.exp(c_last)

        for h in range(GLA_HEADS):
            ks = slice(h * GLA_DK, (h + 1) * GLA_DK)
            q = qkvg_ref[0, rows, pl.ds(h * GLA_DK, GLA_DK)].astype(F32) * scale
            k = qkvg_ref[0, rows, pl.ds(GLA_QK + h * GLA_DK, GLA_DK)].astype(F32)
            v = qkvg_ref[0, rows, pl.ds(2 * GLA_QK + h * GLA_DV, GLA_DV)]
            g = qkvg_ref[0, rows, pl.ds(2 * GLA_QK + GLA_V + h * GLA_DV, GLA_DV)].astype(F32)

            q_intra = (q * e_q[:, ks]).astype(BF16)
            k_intra = (k * e_k[:, ks]).astype(BF16)
            q_inter = (q * e_in[:, ks]).astype(BF16)
            k_out = (k * e_out[:, ks]).astype(BF16)

            scores = jnp.where(causal, _dot_nt(q_intra, k_intra), 0.0).astype(BF16)
            state_t = state_ref[h]
            o = _dot(scores, v) + _dot_nt(q_inter, state_t.astype(BF16))
            state_ref[h] = state_t * e_last[:, ks] + _dot_tn(v, k_out)

            o = o * lax.rsqrt(jnp.mean(o * o, axis=-1, keepdims=True) + LN_EPS) * ng_ref[...]
            o_ref[0, rows, pl.ds(h * GLA_DV, GLA_DV)] = (o * _silu(g)).astype(o_ref.dtype)


def _gla(qkvg, alr, w_gate, b_gate, norm_g, *, t, c):
    bsz, s, _ = qkvg.shape
    assert s % t == 0 and t % c == 0
    return pl.pallas_call(
        functools.partial(_gla_kernel, c=c),
        grid=(bsz, s // t),
        in_specs=[
            pl.BlockSpec((1, t, QKVG_COLS), lambda b, i: (b, i, 0)),
            pl.BlockSpec((1, t, GATE_RANK), lambda b, i: (b, i, 0)),
            pl.BlockSpec((GATE_RANK, GLA_QK), lambda b, i: (0, 0)),
            pl.BlockSpec((1, GLA_QK), lambda b, i: (0, 0)),
            pl.BlockSpec((1, GLA_DV), lambda b, i: (0, 0)),
        ],
        out_specs=pl.BlockSpec((1, t, GLA_V), lambda b, i: (b, i, 0)),
        out_shape=jax.ShapeDtypeStruct((bsz, s, GLA_V), BF16),
        scratch_shapes=[pltpu.VMEM((GLA_HEADS, GLA_DV, GLA_DK), F32)],
        compiler_params=_params(("parallel", "arbitrary")),
        name="gla",
    )(qkvg, alr, w_gate, b_gate, norm_g)


def _conv_kernel(glu_ref, w_ref, b_ref, lg_ref, lb_ref, o_ref, win_ref, acc_ref, *, rb):
    t = glu_ref.shape[1]
    ch = o_ref.shape[2]
    ext = t + CONV_HALO

    @pl.when(pl.program_id(1) == 0)
    def _():
        win_ref[0, pl.ds(0, CONV_HALO), :] = jnp.zeros((CONV_HALO, ch), F32)

    @pl.when(pl.program_id(1) > 0)
    def _():
        win_ref[0, pl.ds(0, CONV_HALO), :] = win_ref[0, pl.ds(t, CONV_HALO), :]

    a = glu_ref[0, :, pl.ds(0, ch)].astype(F32)
    gate = glu_ref[0, :, pl.ds(ch, ch)].astype(F32)
    win_ref[0, pl.ds(CONV_HALO, t), :] = a * jax.nn.sigmoid(gate)
    for r in range(1, SUBLANES):
        win_ref[r, pl.ds(0, ext - SUBLANES), :] = win_ref[0, pl.ds(r, ext - SUBLANES), :]

    first = CONV_HALO - (CONV_WIDTH - 1)
    lane_blk = 256

    def row_block(ib, carry):
        r0 = pl.multiple_of(ib * rb, rb)
        for lb in range(ch // lane_blk):
            lanes = pl.ds(lb * lane_blk, lane_blk)
            acc = jnp.zeros((rb, lane_blk), F32)
            for j in range(CONV_WIDTH):
                off = first + j
                tap = win_ref[off % SUBLANES, pl.ds(r0 + (off // SUBLANES) * SUBLANES, rb), lanes]
                acc = acc + tap * w_ref[pl.ds(j, 1), lanes]
            acc_ref[pl.ds(r0, rb), lanes] = acc
        return carry

    lax.fori_loop(0, t // rb, row_block, 0)

    y = _layer_norm(acc_ref[...] + b_ref[...], lg_ref[...], lb_ref[...])
    o_ref[0] = _silu(y).astype(o_ref.dtype)


def _conv(glu_in, conv_w, conv_b, ln_g, ln_b, *, t, rb):
    bsz, s, two_ch = glu_in.shape
    ch = two_ch // 2
    assert s % t == 0 and t % rb == 0
    return pl.pallas_call(
        functools.partial(_conv_kernel, rb=rb),
        grid=(bsz, s // t),
        in_specs=[
            pl.BlockSpec((1, t, two_ch), lambda b, i: (b, i, 0)),
            pl.BlockSpec((CONV_WIDTH, ch), lambda b, i: (0, 0)),
            pl.BlockSpec((1, ch), lambda b, i: (0, 0)),
            pl.BlockSpec((1, ch), lambda b, i: (0, 0)),
            pl.BlockSpec((1, ch), lambda b, i: (0, 0)),
        ],
        out_specs=pl.BlockSpec((1, t, ch), lambda b, i: (b, i, 0)),
        out_shape=jax.ShapeDtypeStruct((bsz, s, ch), BF16),
        scratch_shapes=[
            pltpu.VMEM((SUBLANES, t + CONV_HALO, ch), F32),
            pltpu.VMEM((t, ch), F32),
        ],
        compiler_params=_params(("parallel", "arbitrary")),
        name="conv",
    )(glu_in, conv_w, conv_b, ln_g, ln_b)


def _mix_out_kernel(x_ref, gla_ref, cv_ref, wa_ref, wb_ref, g_ref, b_ref, o_ref):
    mix = _dot(gla_ref[...], wa_ref[...]) + _dot(cv_ref[...], wb_ref[...])
    o_ref[...] = _layer_norm(ALPHA * x_ref[...] + mix, g_ref[...], b_ref[...])


def _mix_out(x, gla, cv, w, g, b, *, tm):
    m, d = x.shape
    ka = gla.shape[1]
    kb = cv.shape[1]
    assert m % tm == 0 and ka == kb
    return pl.pallas_call(
        _mix_out_kernel,
        grid=(m // tm,),
        in_specs=[
            pl.BlockSpec((tm, d), lambda i: (i, 0)),
            pl.BlockSpec((tm, ka), lambda i: (i, 0)),
            pl.BlockSpec((tm, kb), lambda i: (i, 0)),
            pl.BlockSpec((ka, d), lambda i: (0, 0)),
            pl.BlockSpec((kb, d), lambda i: (1, 0)),
            pl.BlockSpec((1, d), lambda i: (0, 0)),
            pl.BlockSpec((1, d), lambda i: (0, 0)),
        ],
        out_specs=pl.BlockSpec((tm, d), lambda i: (i, 0)),
        out_shape=jax.ShapeDtypeStruct((m, d), F32),
        compiler_params=_params(("parallel",)),
        name="mix_out",
    )(x, gla, cv, w, w, g, b)


def _ple_kernel(x_ref, p_ref, wg_ref, bg_ref, wp_ref, g_ref, b_ref, o_ref):
    x = x_ref[...]
    gate = jax.nn.sigmoid(_dot(x.astype(BF16), wg_ref[...]) + bg_ref[...])
    emb = _dot(p_ref[...].astype(BF16), wp_ref[...])
    o_ref[...] = _layer_norm(ALPHA * x + gate * emb, g_ref[...], b_ref[...])


def _ple(x, p, w_gate, b_gate, w_proj, g, b, *, tm):
    m, d = x.shape
    dp = p.shape[1]
    assert m % tm == 0
    return pl.pallas_call(
        _ple_kernel,
        grid=(m // tm,),
        in_specs=[
            pl.BlockSpec((tm, d), lambda i: (i, 0)),
            pl.BlockSpec((tm, dp), lambda i: (i, 0)),
            pl.BlockSpec((d, d), lambda i: (0, 0)),
            pl.BlockSpec((1, d), lambda i: (0, 0)),
            pl.BlockSpec((dp, d), lambda i: (0, 0)),
            pl.BlockSpec((1, d), lambda i: (0, 0)),
            pl.BlockSpec((1, d), lambda i: (0, 0)),
        ],
        out_specs=pl.BlockSpec((tm, d), lambda i: (i, 0)),
        out_shape=jax.ShapeDtypeStruct((m, d), F32),
        compiler_params=_params(("parallel",)),
        name="ple",
    )(x, p, w_gate, b_gate, w_proj, g, b)


def _layer(x, p, ffn1_w_in, ffn1_w_out, ln_ffn1_g, ln_ffn1_b, w_mix_in, gla_w_gate,
           gla_b_gate, gla_norm_g, conv_w, conv_b, conv_ln_g, conv_ln_b, w_mix_out,
           ln_mix_g, ln_mix_b, ffn2_w_in, ffn2_w_out, ln_ffn2_g, ln_ffn2_b,
           ple_w_gate, ple_b_gate, ple_w_proj, ln_ple_g, ln_ple_b, *, tiles):
    bsz, s, d = x.shape
    m = bsz * s
    row = lambda a: a.reshape(1, -1)
    x2 = x.reshape(m, d)

    x2 = _ffn(x2, ffn1_w_in.astype(BF16), ffn1_w_out.astype(BF16), row(ln_ffn1_g), row(ln_ffn1_b),
              tm=tiles["ffn_tm"], tf=tiles["ffn_tf"])

    w_in = w_mix_in.astype(BF16)
    qkvg = _proj(x2, w_in[:, :QKVG_COLS], BF16, tm=tiles["proj_tm"], tn=tiles["proj_tn"])
    alr = _proj(x2, w_in[:, QKVG_COLS:GLU_START], F32, tm=tiles["proj_tm"], tn=GATE_RANK)
    glu_in = _proj(x2, w_in[:, GLU_START:], BF16, tm=tiles["proj_tm"], tn=tiles["proj_tn"])
    gla = _gla(qkvg.reshape(bsz, s, -1), alr.reshape(bsz, s, -1), gla_w_gate, row(gla_b_gate),
               row(gla_norm_g), t=tiles["gla_t"], c=tiles["gla_c"])
    cv = _conv(glu_in.reshape(bsz, s, -1), conv_w, row(conv_b), row(conv_ln_g), row(conv_ln_b),
               t=tiles["conv_t"], rb=tiles["conv_rb"])
    x2 = _mix_out(x2, gla.reshape(m, -1), cv.reshape(m, -1), w_mix_out.astype(BF16),
                  row(ln_mix_g), row(ln_mix_b), tm=tiles["out_tm"])

    x2 = _ffn(x2, ffn2_w_in.astype(BF16), ffn2_w_out.astype(BF16), row(ln_ffn2_g), row(ln_ffn2_b),
              tm=tiles["ffn_tm"], tf=tiles["ffn_tf"])

    x2 = _ple(x2, p.reshape(m, -1), ple_w_gate.astype(BF16), row(ple_b_gate),
              ple_w_proj.astype(BF16), row(ln_ple_g), row(ln_ple_b), tm=tiles["out_tm"])
    return x2.reshape(bsz, s, d)


TILES = dict(ffn_tm=512, ffn_tf=512, proj_tm=512, proj_tn=1024, gla_t=512, gla_c=128,
             conv_t=512, conv_rb=32, out_tm=512)


def kernel(x, p, ffn1_w_in, ffn1_w_out, ln_ffn1_g, ln_ffn1_b, w_mix_in, gla_w_gate, gla_b_gate, gla_norm_g, conv_w, conv_b, conv_ln_g, conv_ln_b, w_mix_out, ln_mix_g, ln_mix_b, ffn2_w_in, ffn2_w_out, ln_ffn2_g, ln_ffn2_b, ple_w_gate, ple_b_gate, ple_w_proj, ln_ple_g, ln_ple_b):
    assert x.shape[0] == p.shape[1] and ffn1_w_in.shape[0] == DEPTH
    return _layer(x, p[0], ffn1_w_in[0], ffn1_w_out[0], ln_ffn1_g[0], ln_ffn1_b[0], w_mix_in[0],
                  gla_w_gate[0], gla_b_gate[0], gla_norm_g[0], conv_w[0], conv_b[0], conv_ln_g[0],
                  conv_ln_b[0], w_mix_out[0], ln_mix_g[0], ln_mix_b[0], ffn2_w_in[0], ffn2_w_out[0],
                  ln_ffn2_g[0], ln_ffn2_b[0], ple_w_gate[0], ple_b_gate[0], ple_w_proj[0],
                  ln_ple_g[0], ln_ple_b[0], tiles=TILES)
```

```python
import functools

import jax
import jax.numpy as jnp
from jax import lax
from jax.experimental import pallas as pl
from jax.experimental.pallas import tpu as pltpu

DEPTH = 1
GLA_HEADS = 4
GLA_DK = 128
GLA_DV = 256
GLA_QK = GLA_HEADS * GLA_DK
GLA_V = GLA_HEADS * GLA_DV
CONV_CH = 1024
GATE_RANK = 16
GATE_TAU = 16.0
CONV_WIDTH = 31
LN_EPS = 1e-5
ALPHA = (2.0 * DEPTH) ** 0.25

QKVG_COLS = 2 * GLA_QK + 2 * GLA_V
GLU_START = QKVG_COLS + GATE_RANK

SUBLANES = 8
LANES = 128
CONV_HALO = 32

BF16 = jnp.bfloat16
F32 = jnp.float32

VMEM_LIMIT = 56 * 1024 * 1024


def _dot(a, b):
    return jnp.dot(a, b, preferred_element_type=F32)


def _dot_nt(a, b):
    return lax.dot_general(a, b, (((1,), (1,)), ((), ())), preferred_element_type=F32)


def _dot_tn(a, b):
    return lax.dot_general(a, b, (((0,), (0,)), ((), ())), preferred_element_type=F32)


def _layer_norm(y, g, b):
    mu = jnp.mean(y, axis=-1, keepdims=True)
    d = y - mu
    var = jnp.mean(d * d, axis=-1, keepdims=True)
    return d * lax.rsqrt(var + LN_EPS) * g + b


def _silu(x):
    return x * jax.nn.sigmoid(x)


def _params(sem):
    return pltpu.CompilerParams(dimension_semantics=sem, vmem_limit_bytes=VMEM_LIMIT)


def _ffn_kernel(x_ref, wg_ref, wu_ref, wo_ref, g_ref, b_ref, o_ref, xb_ref, act_ref, *, n_ff):
    s = pl.program_id(0)
    n_steps = pl.num_programs(0) - 1
    j_prev = (s + n_ff - 1) % n_ff

    def stage_a(cast_x):
        if cast_x:
            xb_ref[...] = x_ref[...].astype(BF16)
        xb = xb_ref[...]
        gate = _dot(xb, wg_ref[...])
        up = _dot(xb, wu_ref[...])
        act_ref[s % 2] = (0.5 * _silu(gate) * up).astype(BF16)

    def stage_b(first, last):
        part = _dot(act_ref[(s + 1) % 2], wo_ref[...])
        if first:
            acc = ALPHA * x_ref[...] + part
        else:
            acc = o_ref[...] + part
        if last:
            acc = _layer_norm(acc, g_ref[...], b_ref[...])
        o_ref[...] = acc

    first_b = j_prev == 0
    last_b = j_prev == n_ff - 1

    @pl.when(s == 0)
    def _():
        stage_a(True)

    @pl.when(jnp.logical_and(s > 0, first_b))
    def _():
        stage_b(True, False)
        stage_a(False)

    @pl.when(jnp.logical_and(jnp.logical_not(first_b), jnp.logical_not(last_b)))
    def _():
        stage_b(False, False)
        stage_a(False)

    @pl.when(jnp.logical_and(last_b, jnp.logical_and(s > 0, s < n_steps)))
    def _():
        stage_b(False, True)
        stage_a(True)

    @pl.when(s == n_steps)
    def _():
        stage_b(False, True)


def _ffn(x, w_in, w_out, g, b, *, tm, tf):
    m, d = x.shape
    d_ff = w_out.shape[0]
    n_ff = d_ff // tf
    assert m % tm == 0 and d_ff % tf == 0 and n_ff >= 2
    n_steps = (m // tm) * n_ff
    cur = lambda s: jnp.minimum(s, n_steps - 1)
    prev = lambda s: jnp.maximum(s - 1, 0)
    return pl.pallas_call(
        functools.partial(_ffn_kernel, n_ff=n_ff),
        grid=(n_steps + 1,),
        in_specs=[
            pl.BlockSpec((tm, d), lambda s: (cur(s) // n_ff, 0)),
            pl.BlockSpec((d, tf), lambda s: (0, cur(s) % n_ff)),
            pl.BlockSpec((d, tf), lambda s: (0, cur(s) % n_ff + n_ff)),
            pl.BlockSpec((tf, d), lambda s: (prev(s) % n_ff, 0)),
            pl.BlockSpec((1, d), lambda s: (0, 0)),
            pl.BlockSpec((1, d), lambda s: (0, 0)),
        ],
        out_specs=pl.BlockSpec((tm, d), lambda s: (prev(s) // n_ff, 0)),
        out_shape=jax.ShapeDtypeStruct((m, d), F32),
        scratch_shapes=[pltpu.VMEM((tm, d), BF16), pltpu.VMEM((2, tm, tf), BF16)],
        compiler_params=_params(("arbitrary",)),
        name="ffn",
    )(x, w_in, w_in, w_out, g, b)


CONV_RB = 64
PROJ_NB = 768
GLU_NB = 512


def _mix_in_kernel(x_ref, w_ref, cw_ref, cb_ref, lg_ref, lb_ref, qkvg_ref, alr_ref, cv_ref,
                   xb_ref, win_ref, *, tiles_per_seq):
    i = pl.program_id(0)
    t = x_ref.shape[0]
    ch = cv_ref.shape[1]

    @pl.when(i == 0)
    def _():
        win_ref[pl.ds(t, CONV_HALO), :] = jnp.zeros((CONV_HALO, ch), F32)

    tail = win_ref[pl.ds(t, CONV_HALO), :]
    win_ref[pl.ds(0, CONV_HALO), :] = jnp.where(i % tiles_per_seq == 0, 0.0, tail)

    xb_ref[...] = x_ref[...].astype(BF16)
    for c0 in range(0, QKVG_COLS, PROJ_NB):
        cols = pl.ds(c0, PROJ_NB)
        qkvg_ref[:, cols] = _dot(xb_ref[...], w_ref[:, cols]).astype(BF16)
    lr = _dot(xb_ref[...], w_ref[:, pl.ds(QKVG_COLS + 2 * ch, LANES)])
    alr_ref[...] = lr[:, :GATE_RANK]
    for c0 in range(0, ch, GLU_NB):
        a = _dot(xb_ref[...], w_ref[:, pl.ds(QKVG_COLS + c0, GLU_NB)])
        gate = _dot(xb_ref[...], w_ref[:, pl.ds(QKVG_COLS + ch + c0, GLU_NB)])
        win_ref[pl.ds(CONV_HALO, t), pl.ds(c0, GLU_NB)] = a * jax.nn.sigmoid(gate)

    first_off = CONV_HALO - (CONV_WIDTH - 1)
    ext = CONV_RB + CONV_HALO
    for r0 in range(0, t, CONV_RB):
        blocks = []
        for l0 in range(0, ch, LANES):
            lanes = pl.ds(l0, LANES)
            w = win_ref[pl.ds(r0, ext), lanes]
            acc = None
            for r in range(SUBLANES):
                wr = w if r == 0 else pltpu.roll(w, ext - r, 0)
                for off in range(first_off, first_off + CONV_WIDTH):
                    if off % SUBLANES == r:
                        tap = wr[off - r:off - r + CONV_RB] * cw_ref[pl.ds(off - first_off, 1), lanes]
                        acc = tap if acc is None else acc + tap
            blocks.append(acc)
        y = jnp.concatenate(blocks, axis=1) + cb_ref[...]
        y = _layer_norm(y, lg_ref[...], lb_ref[...])
        cv_ref[pl.ds(r0, CONV_RB), :] = _silu(y).astype(cv_ref.dtype)


def _mix_in(x, w, conv_w, conv_b, ln_g, ln_b, *, t, tiles_per_seq):
    m, d = x.shape
    ch = conv_w.shape[1]
    assert m % t == 0 and t % CONV_RB == 0 and ch % LANES == 0
    assert QKVG_COLS % PROJ_NB == 0 and ch % GLU_NB == 0
    assert w.shape[1] == QKVG_COLS + 2 * ch + LANES
    tile = lambda i: (i, 0)
    const = lambda i: (0, 0)
    return pl.pallas_call(
        functools.partial(_mix_in_kernel, tiles_per_seq=tiles_per_seq),
        grid=(m // t,),
        in_specs=[
            pl.BlockSpec((t, d), tile),
            pl.BlockSpec(w.shape, const, pipeline_mode=pl.Buffered(1)),
            pl.BlockSpec((CONV_WIDTH, ch), const),
            pl.BlockSpec((1, ch), const),
            pl.BlockSpec((1, ch), const),
            pl.BlockSpec((1, ch), const),
        ],
        out_specs=[
            pl.BlockSpec((t, QKVG_COLS), tile),
            pl.BlockSpec((t, GATE_RANK), tile),
            pl.BlockSpec((t, ch), tile),
        ],
        out_shape=[
            jax.ShapeDtypeStruct((m, QKVG_COLS), BF16),
            jax.ShapeDtypeStruct((m, GATE_RANK), F32),
            jax.ShapeDtypeStruct((m, ch), BF16),
        ],
        scratch_shapes=[pltpu.VMEM((t, d), BF16), pltpu.VMEM((t + CONV_HALO, ch), F32)],
        compiler_params=_params(("arbitrary",)),
        name="mix_in",
    )(x, w, conv_w, conv_b, ln_g, ln_b)


def _split_bf16(x):
    hi = x.astype(BF16)
    lo = (x - hi.astype(F32)).astype(BF16)
    return hi, lo


def _gla_kernel(qkvg_ref, alr_ref, wgate_ref, bgate_ref, ng_ref, o_ref, state_ref, *, c):
    @pl.when(pl.program_id(1) == 0)
    def _():
        state_ref[...] = jnp.zeros_like(state_ref)

    t = qkvg_ref.shape[1]
    row = lax.broadcasted_iota(jnp.int32, (c, c), 0)
    col = lax.broadcasted_iota(jnp.int32, (c, c), 1)
    causal = row >= col
    tril = causal.astype(BF16)
    scale = GLA_DK ** -0.5

    for blk in range(t // c):
        rows = pl.ds(blk * c, c)
        z = jnp.dot(alr_ref[0, rows, :], wgate_ref[...], preferred_element_type=F32,
                    precision=lax.Precision.HIGHEST) + bgate_ref[...]
        log_a = jax.nn.log_sigmoid(z) / GATE_TAU
        la_hi, la_lo = _split_bf16(log_a)
        cum = _dot(tril, la_hi) + _dot(tril, la_lo)
        c_mid = cum[c // 2 - 1:c // 2, :]
        c_last = cum[c - 1:c, :]
        e_q = jnp.exp(cum - c_mid)
        e_k = jnp.exp(c_mid - cum)
        e_in = jnp.exp(cum)
        e_out = jnp.exp(c_last - cum)
        e_last = jnp.exp(c_last)

        for h in range(GLA_HEADS):
            ks = slice(h * GLA_DK, (h + 1) * GLA_DK)
            q = qkvg_ref[0, rows, pl.ds(h * GLA_DK, GLA_DK)].astype(F32) * scale
            k = qkvg_ref[0, rows, pl.ds(GLA_QK + h * GLA_DK, GLA_DK)].astype(F32)
            v = qkvg_ref[0, rows, pl.ds(2 * GLA_QK + h * GLA_DV, GLA_DV)]
            g = qkvg_ref[0, rows, pl.ds(2 * GLA_QK + GLA_V + h * GLA_DV, GLA_DV)].astype(F32)

            q_intra = (q * e_q[:, ks]).astype(BF16)
            k_intra = (k * e_k[:, ks]).astype(BF16)
            q_inter = (q * e_in[:, ks]).astype(BF16)
            k_out = (k * e_out[:, ks]).astype(BF16)

            scores = jnp.where(causal, _dot_nt(q_intra, k_intra), 0.0).astype(BF16)
            state_t = state_ref[h]
            o = _dot(scores, v) + _dot_nt(q_inter, state_t.astype(BF16))
            state_ref[h] = state_t * e_last[:, ks] + _dot_tn(v, k_out)

            o = o * lax.rsqrt(jnp.mean(o * o, axis=-1, keepdims=True) + LN_EPS) * ng_ref[...]
            o_ref[0, rows, pl.ds(h * GLA_DV, GLA_DV)] = (o * _silu(g)).astype(o_ref.dtype)


def _gla(qkvg, alr, w_gate, b_gate, norm_g, *, t, c):
    bsz, s, _ = qkvg.shape
    assert s % t == 0 and t % c == 0
    return pl.pallas_call(
        functools.partial(_gla_kernel, c=c),
        grid=(bsz, s // t),
        in_specs=[
            pl.BlockSpec((1, t, QKVG_COLS), lambda b, i: (b, i, 0)),
            pl.BlockSpec((1, t, GATE_RANK), lambda b, i: (b, i, 0)),
            pl.BlockSpec((GATE_RANK, GLA_QK), lambda b, i: (0, 0)),
            pl.BlockSpec((1, GLA_QK), lambda b, i: (0, 0)),
            pl.BlockSpec((1, GLA_DV), lambda b, i: (0, 0)),
        ],
        out_specs=pl.BlockSpec((1, t, GLA_V), lambda b, i: (b, i, 0)),
        out_shape=jax.ShapeDtypeStruct((bsz, s, GLA_V), BF16),
        scratch_shapes=[pltpu.VMEM((GLA_HEADS, GLA_DV, GLA_DK), F32)],
        compiler_params=_params(("parallel", "arbitrary")),
        name="gla",
    )(qkvg, alr, w_gate, b_gate, norm_g)


EPILOGUE_ROWS = 256


def _mix_out_kernel(x_ref, gla_ref, cv_ref, wa_ref, wb_ref, g_ref, b_ref, o_ref):
    for r0 in range(0, x_ref.shape[0], EPILOGUE_ROWS):
        rows = pl.ds(r0, EPILOGUE_ROWS)
        mix = _dot(gla_ref[rows, :], wa_ref[...]) + _dot(cv_ref[rows, :], wb_ref[...])
        o_ref[rows, :] = _layer_norm(ALPHA * x_ref[rows, :] + mix, g_ref[...], b_ref[...])


def _mix_out(x, gla, cv, w, g, b, *, tm):
    m, d = x.shape
    ka = gla.shape[1]
    kb = cv.shape[1]
    assert m % tm == 0 and ka == kb
    return pl.pallas_call(
        _mix_out_kernel,
        grid=(m // tm,),
        in_specs=[
            pl.BlockSpec((tm, d), lambda i: (i, 0)),
            pl.BlockSpec((tm, ka), lambda i: (i, 0)),
            pl.BlockSpec((tm, kb), lambda i: (i, 0)),
            pl.BlockSpec((ka, d), lambda i: (0, 0)),
            pl.BlockSpec((kb, d), lambda i: (1, 0)),
            pl.BlockSpec((1, d), lambda i: (0, 0)),
            pl.BlockSpec((1, d), lambda i: (0, 0)),
        ],
        out_specs=pl.BlockSpec((tm, d), lambda i: (i, 0)),
        out_shape=jax.ShapeDtypeStruct((m, d), F32),
        compiler_params=_params(("parallel",)),
        name="mix_out",
    )(x, gla, cv, w, w, g, b)


def _ple_kernel(x_ref, p_ref, wg_ref, bg_ref, wp_ref, g_ref, b_ref, o_ref):
    for r0 in range(0, x_ref.shape[0], EPILOGUE_ROWS):
        rows = pl.ds(r0, EPILOGUE_ROWS)
        x = x_ref[rows, :]
        gate = jax.nn.sigmoid(_dot(x.astype(BF16), wg_ref[...]) + bg_ref[...])
        emb = _dot(p_ref[rows, :].astype(BF16), wp_ref[...])
        o_ref[rows, :] = _layer_norm(ALPHA * x + gate * emb, g_ref[...], b_ref[...])


def _ple(x, p, w_gate, b_gate, w_proj, g, b, *, tm):
    m, d = x.shape
    dp = p.shape[1]
    assert m % tm == 0
    return pl.pallas_call(
        _ple_kernel,
        grid=(m // tm,),
        in_specs=[
            pl.BlockSpec((tm, d), lambda i: (i, 0)),
            pl.BlockSpec((tm, dp), lambda i: (i, 0)),
            pl.BlockSpec((d, d), lambda i: (0, 0)),
            pl.BlockSpec((1, d), lambda i: (0, 0)),
            pl.BlockSpec((dp, d), lambda i: (0, 0)),
            pl.BlockSpec((1, d), lambda i: (0, 0)),
            pl.BlockSpec((1, d), lambda i: (0, 0)),
        ],
        out_specs=pl.BlockSpec((tm, d), lambda i: (i, 0)),
        out_shape=jax.ShapeDtypeStruct((m, d), F32),
        compiler_params=_params(("parallel",)),
        name="ple",
    )(x, p, w_gate, b_gate, w_proj, g, b)


def _layer(x, p, ffn1_w_in, ffn1_w_out, ln_ffn1_g, ln_ffn1_b, w_mix_in, gla_w_gate,
           gla_b_gate, gla_norm_g, conv_w, conv_b, conv_ln_g, conv_ln_b, w_mix_out,
           ln_mix_g, ln_mix_b, ffn2_w_in, ffn2_w_out, ln_ffn2_g, ln_ffn2_b,
           ple_w_gate, ple_b_gate, ple_w_proj, ln_ple_g, ln_ple_b, *, tiles):
    bsz, s, d = x.shape
    m = bsz * s
    row = lambda a: a.reshape(1, -1)
    x2 = x.reshape(m, d)

    x2 = _ffn(x2, ffn1_w_in.astype(BF16), ffn1_w_out.astype(BF16), row(ln_ffn1_g), row(ln_ffn1_b),
              tm=tiles["ffn_tm"], tf=tiles["ffn_tf"])

    w_in = jnp.concatenate(
        [w_mix_in[:, :QKVG_COLS], w_mix_in[:, GLU_START:], w_mix_in[:, QKVG_COLS:GLU_START],
         jnp.zeros((d, LANES - GATE_RANK), w_mix_in.dtype)], axis=1).astype(BF16)
    assert s % tiles["mix_t"] == 0
    qkvg, alr, cv = _mix_in(x2, w_in, conv_w, row(conv_b), row(conv_ln_g), row(conv_ln_b),
                            t=tiles["mix_t"], tiles_per_seq=s // tiles["mix_t"])
    gla = _gla(qkvg.reshape(bsz, s, -1), alr.reshape(bsz, s, -1), gla_w_gate, row(gla_b_gate),
               row(gla_norm_g), t=tiles["gla_t"], c=tiles["gla_c"])
    x2 = _mix_out(x2, gla.reshape(m, -1), cv, w_mix_out.astype(BF16),
                  row(ln_mix_g), row(ln_mix_b), tm=tiles["out_tm"])

    x2 = _ffn(x2, ffn2_w_in.astype(BF16), ffn2_w_out.astype(BF16), row(ln_ffn2_g), row(ln_ffn2_b),
              tm=tiles["ffn_tm"], tf=tiles["ffn_tf"])

    x2 = _ple(x2, p.reshape(m, -1), ple_w_gate.astype(BF16), row(ple_b_gate),
              ple_w_proj.astype(BF16), row(ln_ple_g), row(ln_ple_b), tm=tiles["out_tm"])
    return x2.reshape(bsz, s, d)


TILES = dict(ffn_tm=512, ffn_tf=512, mix_t=512, gla_t=512, gla_c=128, out_tm=512)


def kernel(x, p, ffn1_w_in, ffn1_w_out, ln_ffn1_g, ln_ffn1_b, w_mix_in, gla_w_gate, gla_b_gate, gla_norm_g, conv_w, conv_b, conv_ln_g, conv_ln_b, w_mix_out, ln_mix_g, ln_mix_b, ffn2_w_in, ffn2_w_out, ln_ffn2_g, ln_ffn2_b, ple_w_gate, ple_b_gate, ple_w_proj, ln_ple_g, ln_ple_b):
    assert x.shape[0] == p.shape[1] and ffn1_w_in.shape[0] == DEPTH
    return _layer(x, p[0], ffn1_w_in[0], ffn1_w_out[0], ln_ffn1_g[0], ln_ffn1_b[0], w_mix_in[0],
                  gla_w_gate[0], gla_b_gate[0], gla_norm_g[0], conv_w[0], conv_b[0], conv_ln_g[0],
                  conv_ln_b[0], w_mix_out[0], ln_mix_g[0], ln_mix_b[0], ffn2_w_in[0], ffn2_w_out[0],
                  ln_ffn2_g[0], ln_ffn2_b[0], ple_w_gate[0], ple_b_gate[0], ple_w_proj[0],
                  ln_ple_g[0], ln_ple_b[0], tiles=TILES)
```

```python
import functools

import jax
import jax.numpy as jnp
from jax import lax
from jax.experimental import pallas as pl
from jax.experimental.pallas import tpu as pltpu

DEPTH = 1
GLA_HEADS = 4
GLA_DK = 128
GLA_DV = 256
GLA_QK = GLA_HEADS * GLA_DK
GLA_V = GLA_HEADS * GLA_DV
CONV_CH = 1024
GATE_RANK = 16
GATE_TAU = 16.0
CONV_WIDTH = 31
LN_EPS = 1e-5
ALPHA = (2.0 * DEPTH) ** 0.25

QKVG_COLS = 2 * GLA_QK + 2 * GLA_V
GLU_START = QKVG_COLS + GATE_RANK

SUBLANES = 8
LANES = 128
CONV_HALO = 32

BF16 = jnp.bfloat16
F32 = jnp.float32

VMEM_LIMIT = 56 * 1024 * 1024


def _dot(a, b):
    return jnp.dot(a, b, preferred_element_type=F32)


def _dot_nt(a, b):
    return lax.dot_general(a, b, (((1,), (1,)), ((), ())), preferred_element_type=F32)


def _dot_tn(a, b):
    return lax.dot_general(a, b, (((0,), (0,)), ((), ())), preferred_element_type=F32)


def _layer_norm(y, g, b):
    mu = jnp.mean(y, axis=-1, keepdims=True)
    d = y - mu
    var = jnp.mean(d * d, axis=-1, keepdims=True)
    return d * lax.rsqrt(var + LN_EPS) * g + b


def _silu(x):
    return x * jax.nn.sigmoid(x)


def _params(sem):
    return pltpu.CompilerParams(dimension_semantics=sem, vmem_limit_bytes=VMEM_LIMIT)


def _ffn_kernel(x_ref, wgu_ref, wo_ref, g_ref, b_ref, o_ref, xb_ref, act_ref, *, n_ff):
    s = pl.program_id(0)
    n_steps = pl.num_programs(0) - 1
    j_prev = (s + n_ff - 1) % n_ff

    def stage_a(cast_x):
        if cast_x:
            xb_ref[...] = x_ref[...].astype(BF16)
        tf = act_ref.shape[2]
        gate_up = _dot(xb_ref[...], wgu_ref[0])
        act_ref[s % 2] = (0.5 * _silu(gate_up[:, :tf]) * gate_up[:, tf:]).astype(BF16)

    def stage_b(first, last):
        part = _dot(act_ref[(s + 1) % 2], wo_ref[...])
        if first:
            acc = ALPHA * x_ref[...] + part
        else:
            acc = o_ref[...] + part
        if last:
            acc = _layer_norm(acc, g_ref[...], b_ref[...])
        o_ref[...] = acc

    first_b = j_prev == 0
    last_b = j_prev == n_ff - 1

    @pl.when(s == 0)
    def _():
        stage_a(True)

    @pl.when(jnp.logical_and(s > 0, first_b))
    def _():
        stage_b(True, False)
        stage_a(False)

    @pl.when(jnp.logical_and(jnp.logical_not(first_b), jnp.logical_not(last_b)))
    def _():
        stage_b(False, False)
        stage_a(False)

    @pl.when(jnp.logical_and(last_b, jnp.logical_and(s > 0, s < n_steps)))
    def _():
        stage_b(False, True)
        stage_a(True)

    @pl.when(s == n_steps)
    def _():
        stage_b(False, True)


def _pack_kernel(gate_ref, up_ref, o_ref):
    tf = gate_ref.shape[1]
    o_ref[0, :, pl.ds(0, tf)] = gate_ref[...].astype(BF16)
    o_ref[0, :, pl.ds(tf, tf)] = up_ref[...].astype(BF16)


def _ffn_weights(w_in, w_out, tf):
    d, two_ff = w_in.shape
    d_ff = two_ff // 2
    n_ff = d_ff // tf
    assert d_ff % tf == 0
    w_gu = pl.pallas_call(
        _pack_kernel,
        grid=(n_ff,),
        in_specs=[
            pl.BlockSpec((d, tf), lambda j: (0, j)),
            pl.BlockSpec((d, tf), lambda j: (0, j + n_ff)),
        ],
        out_specs=pl.BlockSpec((1, d, 2 * tf), lambda j: (j, 0, 0)),
        out_shape=jax.ShapeDtypeStruct((n_ff, d, 2 * tf), BF16),
        compiler_params=_params(("parallel",)),
        name="pack_w_in",
    )(w_in, w_in)
    return w_gu, w_out.astype(BF16)


def _ffn(x, w_gu, w_out, g, b, *, tm):
    m, d = x.shape
    n_ff, _, two_tf = w_gu.shape
    tf = two_tf // 2
    assert m % tm == 0 and w_out.shape[0] == n_ff * tf and n_ff >= 2
    n_steps = (m // tm) * n_ff
    cur = lambda s: jnp.minimum(s, n_steps - 1)
    prev = lambda s: jnp.maximum(s - 1, 0)
    return pl.pallas_call(
        functools.partial(_ffn_kernel, n_ff=n_ff),
        grid=(n_steps + 1,),
        in_specs=[
            pl.BlockSpec((tm, d), lambda s: (cur(s) // n_ff, 0)),
            pl.BlockSpec((1, d, two_tf), lambda s: (cur(s) % n_ff, 0, 0)),
            pl.BlockSpec((tf, d), lambda s: (prev(s) % n_ff, 0)),
            pl.BlockSpec((1, d), lambda s: (0, 0)),
            pl.BlockSpec((1, d), lambda s: (0, 0)),
        ],
        out_specs=pl.BlockSpec((tm, d), lambda s: (prev(s) // n_ff, 0)),
        out_shape=jax.ShapeDtypeStruct((m, d), F32),
        scratch_shapes=[pltpu.VMEM((tm, d), BF16), pltpu.VMEM((2, tm, tf), BF16)],
        compiler_params=_params(("arbitrary",)),
        name="ffn",
    )(x, w_gu, w_out, g, b)


CONV_RB = 64
PROJ_NB = 768
GLU_ROWS = 256


def _mix_in_kernel(x_ref, w_ref, cw_ref, cb_ref, lg_ref, lb_ref, qkvg_ref, alr_ref, cv_ref,
                   xb_ref, win_ref, *, tiles_per_seq):
    i = pl.program_id(0)
    t = x_ref.shape[0]
    ch = cv_ref.shape[1]

    @pl.when(i == 0)
    def _():
        win_ref[pl.ds(t, CONV_HALO), :] = jnp.zeros((CONV_HALO, ch), F32)

    tail = win_ref[pl.ds(t, CONV_HALO), :]
    win_ref[pl.ds(0, CONV_HALO), :] = jnp.where(i % tiles_per_seq == 0, 0.0, tail)

    xb_ref[...] = x_ref[...].astype(BF16)
    for c0 in range(0, QKVG_COLS, PROJ_NB):
        cols = pl.ds(c0, PROJ_NB)
        qkvg_ref[:, cols] = _dot(xb_ref[...], w_ref[:, cols]).astype(BF16)
    n_rest = w_ref.shape[1] - QKVG_COLS
    for r0 in range(0, t, GLU_ROWS):
        rows = pl.ds(r0, GLU_ROWS)
        rest = _dot(xb_ref[rows, :], w_ref[:, pl.ds(QKVG_COLS, n_rest)])
        alr_ref[rows, :] = rest[:, :GATE_RANK]
        a = rest[:, GATE_RANK:GATE_RANK + ch]
        gate = rest[:, GATE_RANK + ch:GATE_RANK + 2 * ch]
        win_ref[pl.ds(CONV_HALO + r0, GLU_ROWS), :] = a * jax.nn.sigmoid(gate)

    first_off = CONV_HALO - (CONV_WIDTH - 1)
    ext = CONV_RB + CONV_HALO
    for r0 in range(0, t, CONV_RB):
        blocks = []
        for l0 in range(0, ch, LANES):
            lanes = pl.ds(l0, LANES)
            w = win_ref[pl.ds(r0, ext), lanes]
            acc = None
            for r in range(SUBLANES):
                wr = w if r == 0 else pltpu.roll(w, ext - r, 0)
                for off in range(first_off, first_off + CONV_WIDTH):
                    if off % SUBLANES == r:
                        tap = wr[off - r:off - r + CONV_RB] * cw_ref[pl.ds(off - first_off, 1), lanes]
                        acc = tap if acc is None else acc + tap
            blocks.append(acc)
        y = jnp.concatenate(blocks, axis=1) + cb_ref[...]
        y = _layer_norm(y, lg_ref[...], lb_ref[...])
        cv_ref[pl.ds(r0, CONV_RB), :] = _silu(y).astype(cv_ref.dtype)


def _mix_in(x, w, conv_w, conv_b, ln_g, ln_b, *, t, tiles_per_seq):
    m, d = x.shape
    ch = conv_w.shape[1]
    assert m % t == 0 and t % CONV_RB == 0 and ch % LANES == 0
    assert QKVG_COLS % PROJ_NB == 0 and t % GLU_ROWS == 0
    assert w.shape[1] % LANES == 0 and w.shape[1] >= GLU_START + 2 * ch
    tile = lambda i: (i, 0)
    const = lambda i: (0, 0)
    return pl.pallas_call(
        functools.partial(_mix_in_kernel, tiles_per_seq=tiles_per_seq),
        grid=(m // t,),
        in_specs=[
            pl.BlockSpec((t, d), tile),
            pl.BlockSpec(w.shape, const, pipeline_mode=pl.Buffered(1)),
            pl.BlockSpec((CONV_WIDTH, ch), const),
            pl.BlockSpec((1, ch), const),
            pl.BlockSpec((1, ch), const),
            pl.BlockSpec((1, ch), const),
        ],
        out_specs=[
            pl.BlockSpec((t, QKVG_COLS), tile),
            pl.BlockSpec((t, GATE_RANK), tile),
            pl.BlockSpec((t, ch), tile),
        ],
        out_shape=[
            jax.ShapeDtypeStruct((m, QKVG_COLS), BF16),
            jax.ShapeDtypeStruct((m, GATE_RANK), F32),
            jax.ShapeDtypeStruct((m, ch), BF16),
        ],
        scratch_shapes=[pltpu.VMEM((t, d), BF16), pltpu.VMEM((t + CONV_HALO, ch), F32)],
        compiler_params=_params(("arbitrary",)),
        name="mix_in",
    )(x, w, conv_w, conv_b, ln_g, ln_b)


def _split_bf16(x):
    hi = x.astype(BF16)
    lo = (x - hi.astype(F32)).astype(BF16)
    return hi, lo


def _gla_kernel(qkvg_ref, alr_ref, wgate_ref, bgate_ref, ng_ref, o_ref, state_ref, *, c):
    @pl.when(pl.program_id(1) == 0)
    def _():
        state_ref[...] = jnp.zeros_like(state_ref)

    t = qkvg_ref.shape[1]
    row = lax.broadcasted_iota(jnp.int32, (c, c), 0)
    col = lax.broadcasted_iota(jnp.int32, (c, c), 1)
    causal = row >= col
    tril = causal.astype(BF16)
    scale = GLA_DK ** -0.5

    for blk in range(t // c):
        rows = pl.ds(blk * c, c)
        z = jnp.dot(alr_ref[0, rows, :], wgate_ref[...], preferred_element_type=F32,
                    precision=lax.Precision.HIGHEST) + bgate_ref[...]
        log_a = jax.nn.log_sigmoid(z) / GATE_TAU
        la_hi, la_lo = _split_bf16(log_a)
        cum = _dot(tril, la_hi) + _dot(tril, la_lo)
        c_mid = cum[c // 2 - 1:c // 2, :]
        c_last = cum[c - 1:c, :]
        e_q = jnp.exp(cum - c_mid)
        e_k = jnp.exp(c_mid - cum)
        e_in = jnp.exp(cum)
        e_out = jnp.exp(c_last - cum)
        e_last = jnp.exp(c_last)

        for h in range(GLA_HEADS):
            ks = slice(h * GLA_DK, (h + 1) * GLA_DK)
            q = qkvg_ref[0, rows, pl.ds(h * GLA_DK, GLA_DK)].astype(F32) * scale
            k = qkvg_ref[0, rows, pl.ds(GLA_QK + h * GLA_DK, GLA_DK)].astype(F32)
            v = qkvg_ref[0, rows, pl.ds(2 * GLA_QK + h * GLA_DV, GLA_DV)]
            g = qkvg_ref[0, rows, pl.ds(2 * GLA_QK + GLA_V + h * GLA_DV, GLA_DV)].astype(F32)

            q_intra = (q * e_q[:, ks]).astype(BF16)
            k_intra = (k * e_k[:, ks]).astype(BF16)
            q_inter = (q * e_in[:, ks]).astype(BF16)
            k_out = (k * e_out[:, ks]).astype(BF16)

            scores = jnp.where(causal, _dot_nt(q_intra, k_intra), 0.0).astype(BF16)
            state_t = state_ref[h]
            o = _dot(scores, v) + _dot_nt(q_inter, state_t.astype(BF16))
            state_ref[h] = state_t * e_last[:, ks] + _dot_tn(v, k_out)

            o = o * lax.rsqrt(jnp.mean(o * o, axis=-1, keepdims=True) + LN_EPS) * ng_ref[...]
            o_ref[0, rows, pl.ds(h * GLA_DV, GLA_DV)] = (o * _silu(g)).astype(o_ref.dtype)


def _gla(qkvg, alr, w_gate, b_gate, norm_g, *, t, c):
    bsz, s, _ = qkvg.shape
    assert s % t == 0 and t % c == 0
    return pl.pallas_call(
        functools.partial(_gla_kernel, c=c),
        grid=(bsz, s // t),
        in_specs=[
            pl.BlockSpec((1, t, QKVG_COLS), lambda b, i: (b, i, 0)),
            pl.BlockSpec((1, t, GATE_RANK), lambda b, i: (b, i, 0)),
            pl.BlockSpec((GATE_RANK, GLA_QK), lambda b, i: (0, 0)),
            pl.BlockSpec((1, GLA_QK), lambda b, i: (0, 0)),
            pl.BlockSpec((1, GLA_DV), lambda b, i: (0, 0)),
        ],
        out_specs=pl.BlockSpec((1, t, GLA_V), lambda b, i: (b, i, 0)),
        out_shape=jax.ShapeDtypeStruct((bsz, s, GLA_V), BF16),
        scratch_shapes=[pltpu.VMEM((GLA_HEADS, GLA_DV, GLA_DK), F32)],
        compiler_params=_params(("parallel", "arbitrary")),
        name="gla",
    )(qkvg, alr, w_gate, b_gate, norm_g)


EPILOGUE_ROWS = 256


def _mix_out_kernel(x_ref, gla_ref, cv_ref, wa_ref, wb_ref, g_ref, b_ref, o_ref):
    for r0 in range(0, x_ref.shape[0], EPILOGUE_ROWS):
        rows = pl.ds(r0, EPILOGUE_ROWS)
        mix = _dot(gla_ref[rows, :], wa_ref[...]) + _dot(cv_ref[rows, :], wb_ref[...])
        o_ref[rows, :] = _layer_norm(ALPHA * x_ref[rows, :] + mix, g_ref[...], b_ref[...])


def _mix_out(x, gla, cv, w, g, b, *, tm):
    m, d = x.shape
    ka = gla.shape[1]
    kb = cv.shape[1]
    assert m % tm == 0 and ka == kb
    return pl.pallas_call(
        _mix_out_kernel,
        grid=(m // tm,),
        in_specs=[
            pl.BlockSpec((tm, d), lambda i: (i, 0)),
            pl.BlockSpec((tm, ka), lambda i: (i, 0)),
            pl.BlockSpec((tm, kb), lambda i: (i, 0)),
            pl.BlockSpec((ka, d), lambda i: (0, 0)),
            pl.BlockSpec((kb, d), lambda i: (1, 0)),
            pl.BlockSpec((1, d), lambda i: (0, 0)),
            pl.BlockSpec((1, d), lambda i: (0, 0)),
        ],
        out_specs=pl.BlockSpec((tm, d), lambda i: (i, 0)),
        out_shape=jax.ShapeDtypeStruct((m, d), F32),
        compiler_params=_params(("parallel",)),
        name="mix_out",
    )(x, gla, cv, w, w, g, b)


def _ple_kernel(x_ref, p_ref, wg_ref, bg_ref, wp_ref, g_ref, b_ref, o_ref):
    for r0 in range(0, x_ref.shape[0], EPILOGUE_ROWS):
        rows = pl.ds(r0, EPILOGUE_ROWS)
        x = x_ref[rows, :]
        gate = jax.nn.sigmoid(_dot(x.astype(BF16), wg_ref[...]) + bg_ref[...])
        emb = _dot(p_ref[rows, :].astype(BF16), wp_ref[...])
        o_ref[rows, :] = _layer_norm(ALPHA * x + gate * emb, g_ref[...], b_ref[...])


def _ple(x, p, w_gate, b_gate, w_proj, g, b, *, tm):
    m, d = x.shape
    dp = p.shape[1]
    assert m % tm == 0
    return pl.pallas_call(
        _ple_kernel,
        grid=(m // tm,),
        in_specs=[
            pl.BlockSpec((tm, d), lambda i: (i, 0)),
            pl.BlockSpec((tm, dp), lambda i: (i, 0)),
            pl.BlockSpec((d, d), lambda i: (0, 0)),
            pl.BlockSpec((1, d), lambda i: (0, 0)),
            pl.BlockSpec((dp, d), lambda i: (0, 0)),
            pl.BlockSpec((1, d), lambda i: (0, 0)),
            pl.BlockSpec((1, d), lambda i: (0, 0)),
        ],
        out_specs=pl.BlockSpec((tm, d), lambda i: (i, 0)),
        out_shape=jax.ShapeDtypeStruct((m, d), F32),
        compiler_params=_params(("parallel",)),
        name="ple",
    )(x, p, w_gate, b_gate, w_proj, g, b)


def _layer(x, p, ffn1_w_in, ffn1_w_out, ln_ffn1_g, ln_ffn1_b, w_mix_in, gla_w_gate,
           gla_b_gate, gla_norm_g, conv_w, conv_b, conv_ln_g, conv_ln_b, w_mix_out,
           ln_mix_g, ln_mix_b, ffn2_w_in, ffn2_w_out, ln_ffn2_g, ln_ffn2_b,
           ple_w_gate, ple_b_gate, ple_w_proj, ln_ple_g, ln_ple_b, *, tiles):
    bsz, s, d = x.shape
    m = bsz * s
    row = lambda a: a.reshape(1, -1)
    x2 = x.reshape(m, d)

    x2 = _ffn(x2, *_ffn_weights(ffn1_w_in, ffn1_w_out, tiles["ffn_tf"]), row(ln_ffn1_g), row(ln_ffn1_b),
              tm=tiles["ffn_tm"])

    w_in = jnp.pad(w_mix_in.astype(BF16), ((0, 0), (0, -w_mix_in.shape[1] % LANES)))
    assert s % tiles["mix_t"] == 0
    qkvg, alr, cv = _mix_in(x2, w_in, conv_w, row(conv_b), row(conv_ln_g), row(conv_ln_b),
                            t=tiles["mix_t"], tiles_per_seq=s // tiles["mix_t"])
    gla = _gla(qkvg.reshape(bsz, s, -1), alr.reshape(bsz, s, -1), gla_w_gate, row(gla_b_gate),
               row(gla_norm_g), t=tiles["gla_t"], c=tiles["gla_c"])
    x2 = _mix_out(x2, gla.reshape(m, -1), cv, w_mix_out.astype(BF16),
                  row(ln_mix_g), row(ln_mix_b), tm=tiles["out_tm"])

    x2 = _ffn(x2, *_ffn_weights(ffn2_w_in, ffn2_w_out, tiles["ffn_tf"]), row(ln_ffn2_g), row(ln_ffn2_b),
              tm=tiles["ffn_tm"])

    x2 = _ple(x2, p.reshape(m, -1), ple_w_gate.astype(BF16), row(ple_b_gate),
              ple_w_proj.astype(BF16), row(ln_ple_g), row(ln_ple_b), tm=tiles["out_tm"])
    return x2.reshape(bsz, s, d)


TILES = dict(ffn_tm=1024, ffn_tf=256, mix_t=512, gla_t=512, gla_c=128, out_tm=512)


def kernel(x, p, ffn1_w_in, ffn1_w_out, ln_ffn1_g, ln_ffn1_b, w_mix_in, gla_w_gate, gla_b_gate, gla_norm_g, conv_w, conv_b, conv_ln_g, conv_ln_b, w_mix_out, ln_mix_g, ln_mix_b, ffn2_w_in, ffn2_w_out, ln_ffn2_g, ln_ffn2_b, ple_w_gate, ple_b_gate, ple_w_proj, ln_ple_g, ln_ple_b):
    assert x.shape[0] == p.shape[1] and ffn1_w_in.shape[0] == DEPTH
    return _layer(x, p[0], ffn1_w_in[0], ffn1_w_out[0], ln_ffn1_g[0], ln_ffn1_b[0], w_mix_in[0],
                  gla_w_gate[0], gla_b_gate[0], gla_norm_g[0], conv_w[0], conv_b[0], conv_ln_g[0],
                  conv_ln_b[0], w_mix_out[0], ln_mix_g[0], ln_mix_b[0], ffn2_w_in[0], ffn2_w_out[0],
                  ln_ffn2_g[0], ln_ffn2_b[0], ple_w_gate[0], ple_b_gate[0], ple_w_proj[0],
                  ln_ple_g[0], ln_ple_b[0], tiles=TILES)
```

```python
import functools

import jax
import jax.numpy as jnp
from jax import lax
from jax.experimental import pallas as pl
from jax.experimental.pallas import tpu as pltpu

DEPTH = 1
GLA_HEADS = 4
GLA_DK = 128
GLA_DV = 256
GLA_QK = GLA_HEADS * GLA_DK
GLA_V = GLA_HEADS * GLA_DV
CONV_CH = 1024
GATE_RANK = 16
GATE_TAU = 16.0
CONV_WIDTH = 31
LN_EPS = 1e-5
ALPHA = (2.0 * DEPTH) ** 0.25

QKVG_COLS = 2 * GLA_QK + 2 * GLA_V
GLU_START = QKVG_COLS + GATE_RANK

SUBLANES = 8
LANES = 128
CONV_HALO = 32

BF16 = jnp.bfloat16
F32 = jnp.float32

VMEM_LIMIT = 56 * 1024 * 1024
FFN_VMEM_LIMIT = 62 * 1024 * 1024


def _dot(a, b):
    return jnp.dot(a, b, preferred_element_type=F32)


def _dot_nt(a, b):
    return lax.dot_general(a, b, (((1,), (1,)), ((), ())), preferred_element_type=F32)


def _dot_tn(a, b):
    return lax.dot_general(a, b, (((0,), (0,)), ((), ())), preferred_element_type=F32)


def _layer_norm(y, g, b):
    mu = jnp.mean(y, axis=-1, keepdims=True)
    d = y - mu
    var = jnp.mean(d * d, axis=-1, keepdims=True)
    return d * lax.rsqrt(var + LN_EPS) * g + b


def _silu(x):
    return x * jax.nn.sigmoid(x)


def _params(sem, vmem_limit=VMEM_LIMIT):
    return pltpu.CompilerParams(dimension_semantics=sem, vmem_limit_bytes=vmem_limit)


def _ffn_kernel(x_ref, wgu_ref, wo_ref, g_ref, b_ref, o_ref, xb_ref, act_ref, *, n_ff):
    s = pl.program_id(0)
    n_steps = pl.num_programs(0) - 1
    j_prev = (s + n_ff - 1) % n_ff

    def stage_a(cast_x):
        if cast_x:
            xb_ref[...] = x_ref[...].astype(BF16)
        tf = act_ref.shape[2]
        gate_up = _dot(xb_ref[...], wgu_ref[0])
        act_ref[s % 2] = (0.5 * _silu(gate_up[:, :tf]) * gate_up[:, tf:]).astype(BF16)

    def stage_b(first, last):
        part = _dot(act_ref[(s + 1) % 2], wo_ref[...])
        if first:
            acc = ALPHA * x_ref[...] + part
        else:
            acc = o_ref[...] + part
        if last:
            acc = _layer_norm(acc, g_ref[...], b_ref[...])
        o_ref[...] = acc

    first_b = j_prev == 0
    last_b = j_prev == n_ff - 1

    @pl.when(s == 0)
    def _():
        stage_a(True)

    @pl.when(jnp.logical_and(s > 0, first_b))
    def _():
        stage_b(True, False)
        stage_a(False)

    @pl.when(jnp.logical_and(jnp.logical_not(first_b), jnp.logical_not(last_b)))
    def _():
        stage_b(False, False)
        stage_a(False)

    @pl.when(jnp.logical_and(last_b, jnp.logical_and(s > 0, s < n_steps)))
    def _():
        stage_b(False, True)
        stage_a(True)

    @pl.when(s == n_steps)
    def _():
        stage_b(False, True)


def _pack_kernel(gate_ref, up_ref, o_ref):
    tf = gate_ref.shape[1]
    o_ref[0, :, pl.ds(0, tf)] = gate_ref[...].astype(BF16)
    o_ref[0, :, pl.ds(tf, tf)] = up_ref[...].astype(BF16)


def _ffn_weights(w_in, w_out, tf):
    d, two_ff = w_in.shape
    d_ff = two_ff // 2
    n_ff = d_ff // tf
    assert d_ff % tf == 0
    w_gu = pl.pallas_call(
        _pack_kernel,
        grid=(n_ff,),
        in_specs=[
            pl.BlockSpec((d, tf), lambda j: (0, j)),
            pl.BlockSpec((d, tf), lambda j: (0, j + n_ff)),
        ],
        out_specs=pl.BlockSpec((1, d, 2 * tf), lambda j: (j, 0, 0)),
        out_shape=jax.ShapeDtypeStruct((n_ff, d, 2 * tf), BF16),
        compiler_params=_params(("parallel",)),
        name="pack_w_in",
    )(w_in, w_in)
    return w_gu, w_out.astype(BF16)


def _ffn(x, w_gu, w_out, g, b, *, tm):
    m, d = x.shape
    n_ff, _, two_tf = w_gu.shape
    tf = two_tf // 2
    assert m % tm == 0 and w_out.shape[0] == n_ff * tf and n_ff >= 2
    n_steps = (m // tm) * n_ff
    cur = lambda s: jnp.minimum(s, n_steps - 1)
    prev = lambda s: jnp.maximum(s - 1, 0)
    return pl.pallas_call(
        functools.partial(_ffn_kernel, n_ff=n_ff),
        grid=(n_steps + 1,),
        in_specs=[
            pl.BlockSpec((tm, d), lambda s: (cur(s) // n_ff, 0)),
            pl.BlockSpec((1, d, two_tf), lambda s: (cur(s) % n_ff, 0, 0)),
            pl.BlockSpec((tf, d), lambda s: (prev(s) % n_ff, 0)),
            pl.BlockSpec((1, d), lambda s: (0, 0)),
            pl.BlockSpec((1, d), lambda s: (0, 0)),
        ],
        out_specs=pl.BlockSpec((tm, d), lambda s: (prev(s) // n_ff, 0)),
        out_shape=jax.ShapeDtypeStruct((m, d), F32),
        scratch_shapes=[pltpu.VMEM((tm, d), BF16), pltpu.VMEM((2, tm, tf), BF16)],
        compiler_params=_params(("arbitrary",), FFN_VMEM_LIMIT),
        name="ffn",
    )(x, w_gu, w_out, g, b)


CONV_RB = 64
PROJ_NB = 768
GLU_ROWS = 256


def _mix_in_kernel(x_ref, w_ref, cw_ref, cb_ref, lg_ref, lb_ref, qkvg_ref, alr_ref, cv_ref,
                   xb_ref, win_ref, *, tiles_per_seq):
    i = pl.program_id(0)
    t = x_ref.shape[0]
    ch = cv_ref.shape[1]

    @pl.when(i == 0)
    def _():
        win_ref[pl.ds(t, CONV_HALO), :] = jnp.zeros((CONV_HALO, ch), F32)

    tail = win_ref[pl.ds(t, CONV_HALO), :]
    win_ref[pl.ds(0, CONV_HALO), :] = jnp.where(i % tiles_per_seq == 0, 0.0, tail)

    xb_ref[...] = x_ref[...].astype(BF16)
    for c0 in range(0, QKVG_COLS, PROJ_NB):
        cols = pl.ds(c0, PROJ_NB)
        qkvg_ref[:, cols] = _dot(xb_ref[...], w_ref[:, cols]).astype(BF16)
    n_rest = w_ref.shape[1] - QKVG_COLS
    for r0 in range(0, t, GLU_ROWS):
        rows = pl.ds(r0, GLU_ROWS)
        rest = _dot(xb_ref[rows, :], w_ref[:, pl.ds(QKVG_COLS, n_rest)])
        alr_ref[rows, :] = rest[:, :GATE_RANK]
        a = rest[:, GATE_RANK:GATE_RANK + ch]
        gate = rest[:, GATE_RANK + ch:GATE_RANK + 2 * ch]
        win_ref[pl.ds(CONV_HALO + r0, GLU_ROWS), :] = a * jax.nn.sigmoid(gate)

    first_off = CONV_HALO - (CONV_WIDTH - 1)
    ext = CONV_RB + CONV_HALO
    for r0 in range(0, t, CONV_RB):
        blocks = []
        for l0 in range(0, ch, LANES):
            lanes = pl.ds(l0, LANES)
            w = win_ref[pl.ds(r0, ext), lanes]
            acc = None
            for r in range(SUBLANES):
                wr = w if r == 0 else pltpu.roll(w, ext - r, 0)
                for off in range(first_off, first_off + CONV_WIDTH):
                    if off % SUBLANES == r:
                        tap = wr[off - r:off - r + CONV_RB] * cw_ref[pl.ds(off - first_off, 1), lanes]
                        acc = tap if acc is None else acc + tap
            blocks.append(acc)
        y = jnp.concatenate(blocks, axis=1) + cb_ref[...]
        y = _layer_norm(y, lg_ref[...], lb_ref[...])
        cv_ref[pl.ds(r0, CONV_RB), :] = _silu(y).astype(cv_ref.dtype)


def _mix_in(x, w, conv_w, conv_b, ln_g, ln_b, *, t, tiles_per_seq):
    m, d = x.shape
    ch = conv_w.shape[1]
    assert m % t == 0 and t % CONV_RB == 0 and ch % LANES == 0
    assert QKVG_COLS % PROJ_NB == 0 and t % GLU_ROWS == 0
    assert w.shape[1] % LANES == 0 and w.shape[1] >= GLU_START + 2 * ch
    tile = lambda i: (i, 0)
    const = lambda i: (0, 0)
    return pl.pallas_call(
        functools.partial(_mix_in_kernel, tiles_per_seq=tiles_per_seq),
        grid=(m // t,),
        in_specs=[
            pl.BlockSpec((t, d), tile),
            pl.BlockSpec(w.shape, const, pipeline_mode=pl.Buffered(1)),
            pl.BlockSpec((CONV_WIDTH, ch), const),
            pl.BlockSpec((1, ch), const),
            pl.BlockSpec((1, ch), const),
            pl.BlockSpec((1, ch), const),
        ],
        out_specs=[
            pl.BlockSpec((t, QKVG_COLS), tile),
            pl.BlockSpec((t, GATE_RANK), tile),
            pl.BlockSpec((t, ch), tile),
        ],
        out_shape=[
            jax.ShapeDtypeStruct((m, QKVG_COLS), BF16),
            jax.ShapeDtypeStruct((m, GATE_RANK), F32),
            jax.ShapeDtypeStruct((m, ch), BF16),
        ],
        scratch_shapes=[pltpu.VMEM((t, d), BF16), pltpu.VMEM((t + CONV_HALO, ch), F32)],
        compiler_params=_params(("arbitrary",)),
        name="mix_in",
    )(x, w, conv_w, conv_b, ln_g, ln_b)


def _split_bf16(x):
    hi = x.astype(BF16)
    lo = (x - hi.astype(F32)).astype(BF16)
    return hi, lo


def _gla_kernel(qkvg_ref, alr_ref, wg_hi_ref, wg_lo_ref, bgate_ref, ng_ref, o_ref, state_ref, *, c):
    @pl.when(pl.program_id(1) == 0)
    def _():
        state_ref[...] = jnp.zeros_like(state_ref)

    t = qkvg_ref.shape[1]
    row = lax.broadcasted_iota(jnp.int32, (c, c), 0)
    col = lax.broadcasted_iota(jnp.int32, (c, c), 1)
    causal = row >= col
    tril = causal.astype(BF16)
    scale = GLA_DK ** -0.5

    for blk in range(t // c):
        rows = pl.ds(blk * c, c)
        a_hi, a_lo = _split_bf16(alr_ref[0, rows, :])
        z = (_dot(a_hi, wg_hi_ref[...]) + _dot(a_lo, wg_hi_ref[...]) + _dot(a_hi, wg_lo_ref[...])
             + bgate_ref[...])
        log_a = jax.nn.log_sigmoid(z) / GATE_TAU
        la_hi, la_lo = _split_bf16(log_a)
        cum = _dot(tril, la_hi) + _dot(tril, la_lo)
        c_mid = cum[c // 2 - 1:c // 2, :]
        c_last = cum[c - 1:c, :]
        e_q = jnp.exp(cum - c_mid)
        e_k = jnp.exp(c_mid - cum)
        e_in = jnp.exp(cum)
        e_out = jnp.exp(c_last - cum)
        e_last = jnp.exp(c_last)

        for h in range(GLA_HEADS):
            ks = slice(h * GLA_DK, (h + 1) * GLA_DK)
            q = qkvg_ref[0, rows, pl.ds(h * GLA_DK, GLA_DK)].astype(F32) * scale
            k = qkvg_ref[0, rows, pl.ds(GLA_QK + h * GLA_DK, GLA_DK)].astype(F32)
            v = qkvg_ref[0, rows, pl.ds(2 * GLA_QK + h * GLA_DV, GLA_DV)]
            g = qkvg_ref[0, rows, pl.ds(2 * GLA_QK + GLA_V + h * GLA_DV, GLA_DV)].astype(F32)

            q_intra = (q * e_q[:, ks]).astype(BF16)
            k_intra = (k * e_k[:, ks]).astype(BF16)
            q_inter = (q * e_in[:, ks]).astype(BF16)
            k_out = (k * e_out[:, ks]).astype(BF16)

            scores = jnp.where(causal, _dot_nt(q_intra, k_intra), 0.0).astype(BF16)
            state_t = state_ref[h]
            o = _dot(scores, v) + _dot_nt(q_inter, state_t.astype(BF16))
            state_ref[h] = state_t * e_last[:, ks] + _dot_tn(v, k_out)

            o = o * lax.rsqrt(jnp.mean(o * o, axis=-1, keepdims=True) + LN_EPS) * ng_ref[...]
            o_ref[0, rows, pl.ds(h * GLA_DV, GLA_DV)] = (o * _silu(g)).astype(o_ref.dtype)


def _gla(qkvg, alr, w_gate, b_gate, norm_g, *, t, c):
    bsz, s, _ = qkvg.shape
    assert s % t == 0 and t % c == 0
    wg_hi, wg_lo = _split_bf16(w_gate)
    return pl.pallas_call(
        functools.partial(_gla_kernel, c=c),
        grid=(bsz, s // t),
        in_specs=[
            pl.BlockSpec((1, t, QKVG_COLS), lambda b, i: (b, i, 0)),
            pl.BlockSpec((1, t, GATE_RANK), lambda b, i: (b, i, 0)),
            pl.BlockSpec((GATE_RANK, GLA_QK), lambda b, i: (0, 0)),
            pl.BlockSpec((GATE_RANK, GLA_QK), lambda b, i: (0, 0)),
            pl.BlockSpec((1, GLA_QK), lambda b, i: (0, 0)),
            pl.BlockSpec((1, GLA_DV), lambda b, i: (0, 0)),
        ],
        out_specs=pl.BlockSpec((1, t, GLA_V), lambda b, i: (b, i, 0)),
        out_shape=jax.ShapeDtypeStruct((bsz, s, GLA_V), BF16),
        scratch_shapes=[pltpu.VMEM((GLA_HEADS, GLA_DV, GLA_DK), F32)],
        compiler_params=_params(("parallel", "arbitrary")),
        name="gla",
    )(qkvg, alr, wg_hi, wg_lo, b_gate, norm_g)


EPILOGUE_ROWS = 256


def _mix_out_kernel(x_ref, gla_ref, cv_ref, wa_ref, wb_ref, g_ref, b_ref, o_ref):
    for r0 in range(0, x_ref.shape[0], EPILOGUE_ROWS):
        rows = pl.ds(r0, EPILOGUE_ROWS)
        mix = _dot(gla_ref[rows, :], wa_ref[...]) + _dot(cv_ref[rows, :], wb_ref[...])
        o_ref[rows, :] = _layer_norm(ALPHA * x_ref[rows, :] + mix, g_ref[...], b_ref[...])


def _mix_out(x, gla, cv, w, g, b, *, tm):
    m, d = x.shape
    ka = gla.shape[1]
    kb = cv.shape[1]
    assert m % tm == 0 and ka == kb
    return pl.pallas_call(
        _mix_out_kernel,
        grid=(m // tm,),
        in_specs=[
            pl.BlockSpec((tm, d), lambda i: (i, 0)),
            pl.BlockSpec((tm, ka), lambda i: (i, 0)),
            pl.BlockSpec((tm, kb), lambda i: (i, 0)),
            pl.BlockSpec((ka, d), lambda i: (0, 0)),
            pl.BlockSpec((kb, d), lambda i: (1, 0)),
            pl.BlockSpec((1, d), lambda i: (0, 0)),
            pl.BlockSpec((1, d), lambda i: (0, 0)),
        ],
        out_specs=pl.BlockSpec((tm, d), lambda i: (i, 0)),
        out_shape=jax.ShapeDtypeStruct((m, d), F32),
        compiler_params=_params(("parallel",)),
        name="mix_out",
    )(x, gla, cv, w, w, g, b)


def _ple_kernel(x_ref, p_ref, wg_ref, bg_ref, wp_ref, g_ref, b_ref, o_ref):
    for r0 in range(0, x_ref.shape[0], EPILOGUE_ROWS):
        rows = pl.ds(r0, EPILOGUE_ROWS)
        x = x_ref[rows, :]
        gate = jax.nn.sigmoid(_dot(x.astype(BF16), wg_ref[...]) + bg_ref[...])
        emb = _dot(p_ref[rows, :].astype(BF16), wp_ref[...])
        o_ref[rows, :] = _layer_norm(ALPHA * x + gate * emb, g_ref[...], b_ref[...])


def _ple(x, p, w_gate, b_gate, w_proj, g, b, *, tm):
    m, d = x.shape
    dp = p.shape[1]
    assert m % tm == 0
    return pl.pallas_call(
        _ple_kernel,
        grid=(m // tm,),
        in_specs=[
            pl.BlockSpec((tm, d), lambda i: (i, 0)),
            pl.BlockSpec((tm, dp), lambda i: (i, 0)),
            pl.BlockSpec((d, d), lambda i: (0, 0)),
            pl.BlockSpec((1, d), lambda i: (0, 0)),
            pl.BlockSpec((dp, d), lambda i: (0, 0)),
            pl.BlockSpec((1, d), lambda i: (0, 0)),
            pl.BlockSpec((1, d), lambda i: (0, 0)),
        ],
        out_specs=pl.BlockSpec((tm, d), lambda i: (i, 0)),
        out_shape=jax.ShapeDtypeStruct((m, d), F32),
        compiler_params=_params(("parallel",)),
        name="ple",
    )(x, p, w_gate, b_gate, w_proj, g, b)


def _layer(x, p, ffn1_w_in, ffn1_w_out, ln_ffn1_g, ln_ffn1_b, w_mix_in, gla_w_gate,
           gla_b_gate, gla_norm_g, conv_w, conv_b, conv_ln_g, conv_ln_b, w_mix_out,
           ln_mix_g, ln_mix_b, ffn2_w_in, ffn2_w_out, ln_ffn2_g, ln_ffn2_b,
           ple_w_gate, ple_b_gate, ple_w_proj, ln_ple_g, ln_ple_b, *, tiles):
    bsz, s, d = x.shape
    m = bsz * s
    row = lambda a: a.reshape(1, -1)
    x2 = x.reshape(m, d)

    x2 = _ffn(x2, *_ffn_weights(ffn1_w_in, ffn1_w_out, tiles["ffn_tf"]), row(ln_ffn1_g), row(ln_ffn1_b),
              tm=tiles["ffn_tm"])

    w_in = jnp.pad(w_mix_in.astype(BF16), ((0, 0), (0, -w_mix_in.shape[1] % LANES)))
    assert s % tiles["mix_t"] == 0
    qkvg, alr, cv = _mix_in(x2, w_in, conv_w, row(conv_b), row(conv_ln_g), row(conv_ln_b),
                            t=tiles["mix_t"], tiles_per_seq=s // tiles["mix_t"])
    gla = _gla(qkvg.reshape(bsz, s, -1), alr.reshape(bsz, s, -1), gla_w_gate, row(gla_b_gate),
               row(gla_norm_g), t=tiles["gla_t"], c=tiles["gla_c"])
    x2 = _mix_out(x2, gla.reshape(m, -1), cv, w_mix_out.astype(BF16),
                  row(ln_mix_g), row(ln_mix_b), tm=tiles["out_tm"])

    x2 = _ffn(x2, *_ffn_weights(ffn2_w_in, ffn2_w_out, tiles["ffn_tf"]), row(ln_ffn2_g), row(ln_ffn2_b),
              tm=tiles["ffn_tm"])

    x2 = _ple(x2, p.reshape(m, -1), ple_w_gate.astype(BF16), row(ple_b_gate),
              ple_w_proj.astype(BF16), row(ln_ple_g), row(ln_ple_b), tm=tiles["out_tm"])
    return x2.reshape(bsz, s, d)


TILES = dict(ffn_tm=1024, ffn_tf=512, mix_t=512, gla_t=512, gla_c=128, out_tm=512)


def kernel(x, p, ffn1_w_in, ffn1_w_out, ln_ffn1_g, ln_ffn1_b, w_mix_in, gla_w_gate, gla_b_gate, gla_norm_g, conv_w, conv_b, conv_ln_g, conv_ln_b, w_mix_out, ln_mix_g, ln_mix_b, ffn2_w_in, ffn2_w_out, ln_ffn2_g, ln_ffn2_b, ple_w_gate, ple_b_gate, ple_w_proj, ln_ple_g, ln_ple_b):
    assert x.shape[0] == p.shape[1] and ffn1_w_in.shape[0] == DEPTH
    return _layer(x, p[0], ffn1_w_in[0], ffn1_w_out[0], ln_ffn1_g[0], ln_ffn1_b[0], w_mix_in[0],
                  gla_w_gate[0], gla_b_gate[0], gla_norm_g[0], conv_w[0], conv_b[0], conv_ln_g[0],
                  conv_ln_b[0], w_mix_out[0], ln_mix_g[0], ln_mix_b[0], ffn2_w_in[0], ffn2_w_out[0],
                  ln_ffn2_g[0], ln_ffn2_b[0], ple_w_gate[0], ple_b_gate[0], ple_w_proj[0],
                  ln_ple_g[0], ln_ple_b[0], tiles=TILES)
```

```python
import functools

import jax
import jax.numpy as jnp
from jax import lax
from jax.experimental import pallas as pl
from jax.experimental.pallas import tpu as pltpu

DEPTH = 1
GLA_HEADS = 4
GLA_DK = 128
GLA_DV = 256
GLA_QK = GLA_HEADS * GLA_DK
GLA_V = GLA_HEADS * GLA_DV
CONV_CH = 1024
GATE_RANK = 16
GATE_TAU = 16.0
CONV_WIDTH = 31
LN_EPS = 1e-5
ALPHA = (2.0 * DEPTH) ** 0.25

QKVG_COLS = 2 * GLA_QK + 2 * GLA_V
GLU_START = QKVG_COLS + GATE_RANK

SUBLANES = 8
LANES = 128
CONV_HALO = 32
EPILOGUE_ROWS = 256

BF16 = jnp.bfloat16
F32 = jnp.float32

VMEM_LIMIT = 56 * 1024 * 1024
FFN_VMEM_LIMIT = 62 * 1024 * 1024


def _dot(a, b):
    return jnp.dot(a, b, preferred_element_type=F32)


def _dot_nt(a, b):
    return lax.dot_general(a, b, (((1,), (1,)), ((), ())), preferred_element_type=F32)


def _dot_tn(a, b):
    return lax.dot_general(a, b, (((0,), (0,)), ((), ())), preferred_element_type=F32)


def _layer_norm(y, g, b):
    mu = jnp.mean(y, axis=-1, keepdims=True)
    d = y - mu
    var = jnp.mean(d * d, axis=-1, keepdims=True)
    return d * lax.rsqrt(var + LN_EPS) * g + b


def _silu(x):
    return x * jax.nn.sigmoid(x)


def _params(sem, vmem_limit=VMEM_LIMIT):
    return pltpu.CompilerParams(dimension_semantics=sem, vmem_limit_bytes=vmem_limit)


def _ffn_kernel(x_ref, wgu_ref, wo_ref, g_ref, b_ref, o_ref, xb_ref, act_ref, *, n_ff):
    s = pl.program_id(0)
    n_steps = pl.num_programs(0) - 1
    j_prev = (s + n_ff - 1) % n_ff

    def stage_a(cast_x):
        if cast_x:
            xb_ref[...] = x_ref[...].astype(BF16)
        tf = act_ref.shape[2]
        gate_up = _dot(xb_ref[...], wgu_ref[0])
        act_ref[s % 2] = (0.5 * _silu(gate_up[:, :tf]) * gate_up[:, tf:]).astype(BF16)

    def stage_b(first, last):
        chunk = EPILOGUE_ROWS if last else o_ref.shape[0]
        for r0 in range(0, o_ref.shape[0], chunk):
            rows = pl.ds(r0, chunk)
            part = _dot(act_ref[(s + 1) % 2, rows, :], wo_ref[...])
            if first:
                acc = ALPHA * x_ref[rows, :] + part
            else:
                acc = o_ref[rows, :] + part
            if last:
                acc = _layer_norm(acc, g_ref[...], b_ref[...])
            o_ref[rows, :] = acc

    first_b = j_prev == 0
    last_b = j_prev == n_ff - 1

    @pl.when(s == 0)
    def _():
        stage_a(True)

    @pl.when(jnp.logical_and(s > 0, first_b))
    def _():
        stage_b(True, False)
        stage_a(False)

    @pl.when(jnp.logical_and(jnp.logical_not(first_b), jnp.logical_not(last_b)))
    def _():
        stage_b(False, False)
        stage_a(False)

    @pl.when(jnp.logical_and(last_b, jnp.logical_and(s > 0, s < n_steps)))
    def _():
        stage_b(False, True)
        stage_a(True)

    @pl.when(s == n_steps)
    def _():
        stage_b(False, True)


def _pack_kernel(gate_ref, up_ref, o_ref):
    tf = gate_ref.shape[1]
    o_ref[0, :, pl.ds(0, tf)] = gate_ref[...].astype(BF16)
    o_ref[0, :, pl.ds(tf, tf)] = up_ref[...].astype(BF16)


def _ffn_weights(w_in, w_out, tf):
    d, two_ff = w_in.shape
    d_ff = two_ff // 2
    n_ff = d_ff // tf
    assert d_ff % tf == 0
    w_gu = pl.pallas_call(
        _pack_kernel,
        grid=(n_ff,),
        in_specs=[
            pl.BlockSpec((d, tf), lambda j: (0, j)),
            pl.BlockSpec((d, tf), lambda j: (0, j + n_ff)),
        ],
        out_specs=pl.BlockSpec((1, d, 2 * tf), lambda j: (j, 0, 0)),
        out_shape=jax.ShapeDtypeStruct((n_ff, d, 2 * tf), BF16),
        compiler_params=_params(("parallel",)),
        name="pack_w_in",
    )(w_in, w_in)
    return w_gu, w_out.astype(BF16)


def _ffn(x, w_gu, w_out, g, b, *, tm):
    m, d = x.shape
    n_ff, _, two_tf = w_gu.shape
    tf = two_tf // 2
    assert m % tm == 0 and w_out.shape[0] == n_ff * tf and n_ff >= 2
    n_steps = (m // tm) * n_ff
    cur = lambda s: jnp.minimum(s, n_steps - 1)
    prev = lambda s: jnp.maximum(s - 1, 0)
    return pl.pallas_call(
        functools.partial(_ffn_kernel, n_ff=n_ff),
        grid=(n_steps + 1,),
        in_specs=[
            pl.BlockSpec((tm, d), lambda s: (cur(s) // n_ff, 0)),
            pl.BlockSpec((1, d, two_tf), lambda s: (cur(s) % n_ff, 0, 0)),
            pl.BlockSpec((tf, d), lambda s: (prev(s) % n_ff, 0)),
            pl.BlockSpec((1, d), lambda s: (0, 0)),
            pl.BlockSpec((1, d), lambda s: (0, 0)),
        ],
        out_specs=pl.BlockSpec((tm, d), lambda s: (prev(s) // n_ff, 0)),
        out_shape=jax.ShapeDtypeStruct((m, d), F32),
        scratch_shapes=[pltpu.VMEM((tm, d), BF16), pltpu.VMEM((2, tm, tf), BF16)],
        compiler_params=_params(("arbitrary",), FFN_VMEM_LIMIT),
        name="ffn",
    )(x, w_gu, w_out, g, b)


CONV_RB = 64
PROJ_NB = 768
GLU_ROWS = 256


def _pad_cast_kernel(w_ref, o_ref):
    rows, n_pad = o_ref.shape
    o_ref[:, pl.ds(n_pad - LANES, LANES)] = jnp.zeros((rows, LANES), o_ref.dtype)
    o_ref[:, pl.ds(0, w_ref.shape[1])] = w_ref[...].astype(o_ref.dtype)


def _pad_cast(w, *, rows):
    d, n = w.shape
    n_pad = n + (-n % LANES)
    assert d % rows == 0 and n_pad > n
    return pl.pallas_call(
        _pad_cast_kernel,
        grid=(d // rows,),
        in_specs=[pl.BlockSpec((rows, n), lambda i: (i, 0))],
        out_specs=pl.BlockSpec((rows, n_pad), lambda i: (i, 0)),
        out_shape=jax.ShapeDtypeStruct((d, n_pad), BF16),
        compiler_params=_params(("parallel",)),
        name="pad_cast",
    )(w)


def _mix_in_kernel(x_ref, w_ref, cw_ref, cb_ref, lg_ref, lb_ref, qkvg_ref, alr_ref, cv_ref,
                   xb_ref, win_ref, *, tiles_per_seq):
    i = pl.program_id(0)
    t = x_ref.shape[0]
    ch = cv_ref.shape[1]

    @pl.when(i == 0)
    def _():
        win_ref[pl.ds(t, CONV_HALO), :] = jnp.zeros((CONV_HALO, ch), F32)

    tail = win_ref[pl.ds(t, CONV_HALO), :]
    win_ref[pl.ds(0, CONV_HALO), :] = jnp.where(i % tiles_per_seq == 0, 0.0, tail)

    xb_ref[...] = x_ref[...].astype(BF16)
    for c0 in range(0, QKVG_COLS, PROJ_NB):
        cols = pl.ds(c0, PROJ_NB)
        qkvg_ref[:, cols] = _dot(xb_ref[...], w_ref[:, cols]).astype(BF16)
    n_rest = w_ref.shape[1] - QKVG_COLS
    for r0 in range(0, t, GLU_ROWS):
        rows = pl.ds(r0, GLU_ROWS)
        rest = _dot(xb_ref[rows, :], w_ref[:, pl.ds(QKVG_COLS, n_rest)])
        alr_ref[rows, :] = rest[:, :GATE_RANK]
        a = rest[:, GATE_RANK:GATE_RANK + ch]
        gate = rest[:, GATE_RANK + ch:GATE_RANK + 2 * ch]
        win_ref[pl.ds(CONV_HALO + r0, GLU_ROWS), :] = a * jax.nn.sigmoid(gate)

    first_off = CONV_HALO - (CONV_WIDTH - 1)
    ext = CONV_RB + CONV_HALO
    for r0 in range(0, t, CONV_RB):
        blocks = []
        for l0 in range(0, ch, LANES):
            lanes = pl.ds(l0, LANES)
            w = win_ref[pl.ds(r0, ext), lanes]
            acc = None
            for r in range(SUBLANES):
                wr = w if r == 0 else pltpu.roll(w, ext - r, 0)
                for off in range(first_off, first_off + CONV_WIDTH):
                    if off % SUBLANES == r:
                        tap = wr[off - r:off - r + CONV_RB] * cw_ref[pl.ds(off - first_off, 1), lanes]
                        acc = tap if acc is None else acc + tap
            blocks.append(acc)
        y = jnp.concatenate(blocks, axis=1) + cb_ref[...]
        y = _layer_norm(y, lg_ref[...], lb_ref[...])
        cv_ref[pl.ds(r0, CONV_RB), :] = _silu(y).astype(cv_ref.dtype)


def _mix_in(x, w, conv_w, conv_b, ln_g, ln_b, *, t, tiles_per_seq):
    m, d = x.shape
    ch = conv_w.shape[1]
    assert m % t == 0 and t % CONV_RB == 0 and ch % LANES == 0
    assert QKVG_COLS % PROJ_NB == 0 and t % GLU_ROWS == 0
    assert w.shape[1] % LANES == 0 and w.shape[1] >= GLU_START + 2 * ch
    tile = lambda i: (i, 0)
    const = lambda i: (0, 0)
    return pl.pallas_call(
        functools.partial(_mix_in_kernel, tiles_per_seq=tiles_per_seq),
        grid=(m // t,),
        in_specs=[
            pl.BlockSpec((t, d), tile),
            pl.BlockSpec(w.shape, const, pipeline_mode=pl.Buffered(1)),
            pl.BlockSpec((CONV_WIDTH, ch), const),
            pl.BlockSpec((1, ch), const),
            pl.BlockSpec((1, ch), const),
            pl.BlockSpec((1, ch), const),
        ],
        out_specs=[
            pl.BlockSpec((t, QKVG_COLS), tile),
            pl.BlockSpec((t, GATE_RANK), tile),
            pl.BlockSpec((t, ch), tile),
        ],
        out_shape=[
            jax.ShapeDtypeStruct((m, QKVG_COLS), BF16),
            jax.ShapeDtypeStruct((m, GATE_RANK), F32),
            jax.ShapeDtypeStruct((m, ch), BF16),
        ],
        scratch_shapes=[pltpu.VMEM((t, d), BF16), pltpu.VMEM((t + CONV_HALO, ch), F32)],
        compiler_params=_params(("arbitrary",)),
        name="mix_in",
    )(x, w, conv_w, conv_b, ln_g, ln_b)


def _split_bf16(x):
    hi = x.astype(BF16)
    lo = (x - hi.astype(F32)).astype(BF16)
    return hi, lo


def _mix_tail_kernel(qkvg_ref, alr_ref, wg_hi_ref, wg_lo_ref, bgate_ref, ng_ref, x_ref, cv_ref,
                     wa_ref, wb_ref, lg_ref, lb_ref, o_ref, state_ref, gla_ref, *, c):
    @pl.when(pl.program_id(1) == 0)
    def _():
        state_ref[...] = jnp.zeros_like(state_ref)

    t = qkvg_ref.shape[1]
    row = lax.broadcasted_iota(jnp.int32, (c, c), 0)
    col = lax.broadcasted_iota(jnp.int32, (c, c), 1)
    causal = row >= col
    tril = causal.astype(BF16)
    scale = GLA_DK ** -0.5

    for blk in range(t // c):
        rows = pl.ds(blk * c, c)
        a_hi, a_lo = _split_bf16(alr_ref[0, rows, :])
        z = (_dot(a_hi, wg_hi_ref[...]) + _dot(a_lo, wg_hi_ref[...]) + _dot(a_hi, wg_lo_ref[...])
             + bgate_ref[...])
        log_a = jax.nn.log_sigmoid(z) / GATE_TAU
        la_hi, la_lo = _split_bf16(log_a)
        cum = _dot(tril, la_hi) + _dot(tril, la_lo)
        c_mid = cum[c // 2 - 1:c // 2, :]
        c_last = cum[c - 1:c, :]
        e_q = jnp.exp(cum - c_mid)
        e_k = jnp.exp(c_mid - cum)
        e_in = jnp.exp(cum)
        e_out = jnp.exp(c_last - cum)
        e_last = jnp.exp(c_last)

        for h in range(GLA_HEADS):
            ks = slice(h * GLA_DK, (h + 1) * GLA_DK)
            q = qkvg_ref[0, rows, pl.ds(h * GLA_DK, GLA_DK)].astype(F32) * scale
            k = qkvg_ref[0, rows, pl.ds(GLA_QK + h * GLA_DK, GLA_DK)].astype(F32)
            v = qkvg_ref[0, rows, pl.ds(2 * GLA_QK + h * GLA_DV, GLA_DV)]
            g = qkvg_ref[0, rows, pl.ds(2 * GLA_QK + GLA_V + h * GLA_DV, GLA_DV)].astype(F32)

            q_intra = (q * e_q[:, ks]).astype(BF16)
            k_intra = (k * e_k[:, ks]).astype(BF16)
            q_inter = (q * e_in[:, ks]).astype(BF16)
            k_out = (k * e_out[:, ks]).astype(BF16)

            scores = jnp.where(causal, _dot_nt(q_intra, k_intra), 0.0).astype(BF16)
            state_t = state_ref[h]
            o = _dot(scores, v) + _dot_nt(q_inter, state_t.astype(BF16))
            state_ref[h] = state_t * e_last[:, ks] + _dot_tn(v, k_out)

            o = o * lax.rsqrt(jnp.mean(o * o, axis=-1, keepdims=True) + LN_EPS) * ng_ref[...]
            gla_ref[rows, pl.ds(h * GLA_DV, GLA_DV)] = (o * _silu(g)).astype(gla_ref.dtype)

    for r0 in range(0, t, EPILOGUE_ROWS):
        rows = pl.ds(r0, EPILOGUE_ROWS)
        mix = _dot(gla_ref[rows, :], wa_ref[...]) + _dot(cv_ref[0, rows, :], wb_ref[...])
        o_ref[0, rows, :] = _layer_norm(ALPHA * x_ref[0, rows, :] + mix, lg_ref[...], lb_ref[...])


def _mix_tail(qkvg, alr, w_gate, b_gate, norm_g, x, cv, w_out, ln_g, ln_b, *, t, c):
    bsz, s, d = x.shape
    assert s % t == 0 and t % c == 0 and t % EPILOGUE_ROWS == 0
    assert w_out.shape == (GLA_V + cv.shape[2], d) and cv.shape[2] == GLA_V
    wg_hi, wg_lo = _split_bf16(w_gate)
    tile = lambda b, i: (b, i, 0)
    const = lambda b, i: (0, 0)
    return pl.pallas_call(
        functools.partial(_mix_tail_kernel, c=c),
        grid=(bsz, s // t),
        in_specs=[
            pl.BlockSpec((1, t, QKVG_COLS), tile),
            pl.BlockSpec((1, t, GATE_RANK), tile),
            pl.BlockSpec((GATE_RANK, GLA_QK), const),
            pl.BlockSpec((GATE_RANK, GLA_QK), const),
            pl.BlockSpec((1, GLA_QK), const),
            pl.BlockSpec((1, GLA_DV), const),
            pl.BlockSpec((1, t, d), tile),
            pl.BlockSpec((1, t, GLA_V), tile),
            pl.BlockSpec((GLA_V, d), const, pipeline_mode=pl.Buffered(1)),
            pl.BlockSpec((GLA_V, d), lambda b, i: (1, 0), pipeline_mode=pl.Buffered(1)),
            pl.BlockSpec((1, d), const),
            pl.BlockSpec((1, d), const),
        ],
        out_specs=pl.BlockSpec((1, t, d), tile),
        out_shape=jax.ShapeDtypeStruct((bsz, s, d), F32),
        scratch_shapes=[pltpu.VMEM((GLA_HEADS, GLA_DV, GLA_DK), F32), pltpu.VMEM((t, GLA_V), BF16)],
        compiler_params=_params(("parallel", "arbitrary")),
        name="mix_tail",
    )(qkvg, alr, wg_hi, wg_lo, b_gate, norm_g, x, cv, w_out, w_out, ln_g, ln_b)


def _ple_kernel(x_ref, p_ref, wg_ref, bg_ref, wp_ref, g_ref, b_ref, o_ref):
    for r0 in range(0, x_ref.shape[0], EPILOGUE_ROWS):
        rows = pl.ds(r0, EPILOGUE_ROWS)
        x = x_ref[rows, :]
        gate = jax.nn.sigmoid(_dot(x.astype(BF16), wg_ref[...]) + bg_ref[...])
        emb = _dot(p_ref[rows, :].astype(BF16), wp_ref[...])
        o_ref[rows, :] = _layer_norm(ALPHA * x + gate * emb, g_ref[...], b_ref[...])


def _ple(x, p, w_gate, b_gate, w_proj, g, b, *, tm):
    m, d = x.shape
    dp = p.shape[1]
    assert m % tm == 0
    return pl.pallas_call(
        _ple_kernel,
        grid=(m // tm,),
        in_specs=[
            pl.BlockSpec((tm, d), lambda i: (i, 0)),
            pl.BlockSpec((tm, dp), lambda i: (i, 0)),
            pl.BlockSpec((d, d), lambda i: (0, 0), pipeline_mode=pl.Buffered(1)),
            pl.BlockSpec((1, d), lambda i: (0, 0)),
            pl.BlockSpec((dp, d), lambda i: (0, 0), pipeline_mode=pl.Buffered(1)),
            pl.BlockSpec((1, d), lambda i: (0, 0)),
            pl.BlockSpec((1, d), lambda i: (0, 0)),
        ],
        out_specs=pl.BlockSpec((tm, d), lambda i: (i, 0)),
        out_shape=jax.ShapeDtypeStruct((m, d), F32),
        compiler_params=_params(("parallel",)),
        name="ple",
    )(x, p, w_gate, b_gate, w_proj, g, b)


def _layer(x, p, ffn1_w_in, ffn1_w_out, ln_ffn1_g, ln_ffn1_b, w_mix_in, gla_w_gate,
           gla_b_gate, gla_norm_g, conv_w, conv_b, conv_ln_g, conv_ln_b, w_mix_out,
           ln_mix_g, ln_mix_b, ffn2_w_in, ffn2_w_out, ln_ffn2_g, ln_ffn2_b,
           ple_w_gate, ple_b_gate, ple_w_proj, ln_ple_g, ln_ple_b, *, tiles):
    bsz, s, d = x.shape
    m = bsz * s
    row = lambda a: a.reshape(1, -1)
    x2 = x.reshape(m, d)

    x2 = _ffn(x2, *_ffn_weights(ffn1_w_in, ffn1_w_out, tiles["ffn_tf"]), row(ln_ffn1_g), row(ln_ffn1_b),
              tm=tiles["ffn_tm"])

    w_in = _pad_cast(w_mix_in, rows=tiles["pad_rows"])
    assert s % tiles["mix_t"] == 0
    qkvg, alr, cv = _mix_in(x2, w_in, conv_w, row(conv_b), row(conv_ln_g), row(conv_ln_b),
                            t=tiles["mix_t"], tiles_per_seq=s // tiles["mix_t"])
    x2 = _mix_tail(qkvg.reshape(bsz, s, -1), alr.reshape(bsz, s, -1), gla_w_gate, row(gla_b_gate),
                   row(gla_norm_g), x2.reshape(bsz, s, d), cv.reshape(bsz, s, -1),
                   w_mix_out.astype(BF16), row(ln_mix_g), row(ln_mix_b),
                   t=tiles["gla_t"], c=tiles["gla_c"]).reshape(m, d)

    x2 = _ffn(x2, *_ffn_weights(ffn2_w_in, ffn2_w_out, tiles["ffn_tf"]), row(ln_ffn2_g), row(ln_ffn2_b),
              tm=tiles["ffn_tm"])

    x2 = _ple(x2, p.reshape(m, -1), ple_w_gate.astype(BF16), row(ple_b_gate),
              ple_w_proj.astype(BF16), row(ln_ple_g), row(ln_ple_b), tm=tiles["out_tm"])
    return x2.reshape(bsz, s, d)


TILES = dict(ffn_tm=1024, ffn_tf=512, pad_rows=256, mix_t=512, gla_t=512, gla_c=128, out_tm=1024)


def kernel(x, p, ffn1_w_in, ffn1_w_out, ln_ffn1_g, ln_ffn1_b, w_mix_in, gla_w_gate, gla_b_gate, gla_norm_g, conv_w, conv_b, conv_ln_g, conv_ln_b, w_mix_out, ln_mix_g, ln_mix_b, ffn2_w_in, ffn2_w_out, ln_ffn2_g, ln_ffn2_b, ple_w_gate, ple_b_gate, ple_w_proj, ln_ple_g, ln_ple_b):
    assert x.shape[0] == p.shape[1] and ffn1_w_in.shape[0] == DEPTH
    return _layer(x, p[0], ffn1_w_in[0], ffn1_w_out[0], ln_ffn1_g[0], ln_ffn1_b[0], w_mix_in[0],
                  gla_w_gate[0], gla_b_gate[0], gla_norm_g[0], conv_w[0], conv_b[0], conv_ln_g[0],
                  conv_ln_b[0], w_mix_out[0], ln_mix_g[0], ln_mix_b[0], ffn2_w_in[0], ffn2_w_out[0],
                  ln_ffn2_g[0], ln_ffn2_b[0], ple_w_gate[0], ple_b_gate[0], ple_w_proj[0],
                  ln_ple_g[0], ln_ple_b[0], tiles=TILES)
```

```python
import functools

import jax
import jax.numpy as jnp
from jax import lax
from jax.experimental import pallas as pl
from jax.experimental.pallas import tpu as pltpu

DEPTH = 1
GLA_HEADS = 4
GLA_DK = 128
GLA_DV = 256
GLA_QK = GLA_HEADS * GLA_DK
GLA_V = GLA_HEADS * GLA_DV
CONV_CH = 1024
GATE_RANK = 16
GATE_TAU = 16.0
CONV_WIDTH = 31
LN_EPS = 1e-5
ALPHA = (2.0 * DEPTH) ** 0.25

QKVG_COLS = 2 * GLA_QK + 2 * GLA_V
GLU_START = QKVG_COLS + GATE_RANK

SUBLANES = 8
LANES = 128
CONV_HALO = 32
EPILOGUE_ROWS = 256

BF16 = jnp.bfloat16
F32 = jnp.float32

VMEM_LIMIT = 56 * 1024 * 1024
FFN_VMEM_LIMIT = 62 * 1024 * 1024


def _dot(a, b):
    return jnp.dot(a, b, preferred_element_type=F32)


def _dot_nt(a, b):
    return lax.dot_general(a, b, (((1,), (1,)), ((), ())), preferred_element_type=F32)


def _dot_tn(a, b):
    return lax.dot_general(a, b, (((0,), (0,)), ((), ())), preferred_element_type=F32)


def _layer_norm(y, g, b):
    mu = jnp.mean(y, axis=-1, keepdims=True)
    d = y - mu
    var = jnp.mean(d * d, axis=-1, keepdims=True)
    return d * lax.rsqrt(var + LN_EPS) * g + b


def _silu(x):
    return x * jax.nn.sigmoid(x)


def _params(sem, vmem_limit=VMEM_LIMIT):
    return pltpu.CompilerParams(dimension_semantics=sem, vmem_limit_bytes=vmem_limit)


def _ffn_kernel(x_ref, wgu_ref, wo_ref, g_ref, b_ref, o_ref, xb_ref, act_ref, *, n_ff):
    s = pl.program_id(0)
    n_steps = pl.num_programs(0) - 1
    j_prev = (s + n_ff - 1) % n_ff

    def stage_a(cast_x):
        if cast_x:
            xb_ref[...] = x_ref[...].astype(BF16)
        tf = act_ref.shape[2]
        gate_up = _dot(xb_ref[...], wgu_ref[0])
        act_ref[s % 2] = (0.5 * _silu(gate_up[:, :tf]) * gate_up[:, tf:]).astype(BF16)

    def stage_b(first, last):
        chunk = EPILOGUE_ROWS if last else o_ref.shape[0]
        wo = wo_ref[...].astype(BF16)
        for r0 in range(0, o_ref.shape[0], chunk):
            rows = pl.ds(r0, chunk)
            part = _dot(act_ref[(s + 1) % 2, rows, :], wo)
            if first:
                acc = ALPHA * x_ref[rows, :] + part
            else:
                acc = o_ref[rows, :] + part
            if last:
                acc = _layer_norm(acc, g_ref[...], b_ref[...])
            o_ref[rows, :] = acc

    first_b = j_prev == 0
    last_b = j_prev == n_ff - 1

    @pl.when(s == 0)
    def _():
        stage_a(True)

    @pl.when(jnp.logical_and(s > 0, first_b))
    def _():
        stage_b(True, False)
        stage_a(False)

    @pl.when(jnp.logical_and(jnp.logical_not(first_b), jnp.logical_not(last_b)))
    def _():
        stage_b(False, False)
        stage_a(False)

    @pl.when(jnp.logical_and(last_b, jnp.logical_and(s > 0, s < n_steps)))
    def _():
        stage_b(False, True)
        stage_a(True)

    @pl.when(s == n_steps)
    def _():
        stage_b(False, True)


def _pack_kernel(gate_ref, up_ref, o_ref):
    tf = gate_ref.shape[1]
    o_ref[0, :, pl.ds(0, tf)] = gate_ref[...].astype(BF16)
    o_ref[0, :, pl.ds(tf, tf)] = up_ref[...].astype(BF16)


def _ffn_weights(w_in, w_out, tf):
    d, two_ff = w_in.shape
    d_ff = two_ff // 2
    n_ff = d_ff // tf
    assert d_ff % tf == 0
    w_gu = pl.pallas_call(
        _pack_kernel,
        grid=(n_ff,),
        in_specs=[
            pl.BlockSpec((d, tf), lambda j: (0, j)),
            pl.BlockSpec((d, tf), lambda j: (0, j + n_ff)),
        ],
        out_specs=pl.BlockSpec((1, d, 2 * tf), lambda j: (j, 0, 0)),
        out_shape=jax.ShapeDtypeStruct((n_ff, d, 2 * tf), BF16),
        compiler_params=_params(("parallel",)),
        name="pack_w_in",
    )(w_in, w_in)
    return w_gu, w_out


def _ffn(x, w_gu, w_out, g, b, *, tm):
    m, d = x.shape
    n_ff, _, two_tf = w_gu.shape
    tf = two_tf // 2
    assert m % tm == 0 and w_out.shape[0] == n_ff * tf and n_ff >= 2
    n_steps = (m // tm) * n_ff
    cur = lambda s: jnp.minimum(s, n_steps - 1)
    prev = lambda s: jnp.maximum(s - 1, 0)
    return pl.pallas_call(
        functools.partial(_ffn_kernel, n_ff=n_ff),
        grid=(n_steps + 1,),
        in_specs=[
            pl.BlockSpec((tm, d), lambda s: (cur(s) // n_ff, 0)),
            pl.BlockSpec((1, d, two_tf), lambda s: (cur(s) % n_ff, 0, 0)),
            pl.BlockSpec((tf, d), lambda s: (prev(s) % n_ff, 0)),
            pl.BlockSpec((1, d), lambda s: (0, 0)),
            pl.BlockSpec((1, d), lambda s: (0, 0)),
        ],
        out_specs=pl.BlockSpec((tm, d), lambda s: (prev(s) // n_ff, 0)),
        out_shape=jax.ShapeDtypeStruct((m, d), F32),
        scratch_shapes=[pltpu.VMEM((tm, d), BF16), pltpu.VMEM((2, tm, tf), BF16)],
        compiler_params=_params(("arbitrary",), FFN_VMEM_LIMIT),
        name="ffn",
    )(x, w_gu, w_out, g, b)


CONV_RB = 64
PROJ_NB = 768
GLU_ROWS = 256


def _mix_in_kernel(x_ref, w_ref, cw_ref, cb_ref, lg_ref, lb_ref, qkvg_ref, alr_ref, cv_ref,
                   xb_ref, win_ref, *, tiles_per_seq):
    i = pl.program_id(0)
    t = x_ref.shape[0]
    ch = cv_ref.shape[1]

    @pl.when(i == 0)
    def _():
        win_ref[pl.ds(t, CONV_HALO), :] = jnp.zeros((CONV_HALO, ch), F32)

    tail = win_ref[pl.ds(t, CONV_HALO), :]
    win_ref[pl.ds(0, CONV_HALO), :] = jnp.where(i % tiles_per_seq == 0, 0.0, tail)

    xb_ref[...] = x_ref[...].astype(BF16)
    for c0 in range(0, QKVG_COLS, PROJ_NB):
        cols = pl.ds(c0, PROJ_NB)
        qkvg_ref[:, cols] = _dot(xb_ref[...], w_ref[:, cols]).astype(BF16)
    n_rest = w_ref.shape[1] - QKVG_COLS
    for r0 in range(0, t, GLU_ROWS):
        rows = pl.ds(r0, GLU_ROWS)
        rest = _dot(xb_ref[rows, :], w_ref[:, pl.ds(QKVG_COLS, n_rest)])
        alr_ref[rows, :] = rest[:, :GATE_RANK]
        a = rest[:, GATE_RANK:GATE_RANK + ch]
        gate = rest[:, GATE_RANK + ch:GATE_RANK + 2 * ch]
        win_ref[pl.ds(CONV_HALO + r0, GLU_ROWS), :] = a * jax.nn.sigmoid(gate)

    first_off = CONV_HALO - (CONV_WIDTH - 1)
    ext = CONV_RB + CONV_HALO
    for r0 in range(0, t, CONV_RB):
        blocks = []
        for l0 in range(0, ch, LANES):
            lanes = pl.ds(l0, LANES)
            w = win_ref[pl.ds(r0, ext), lanes]
            acc = None
            for r in range(SUBLANES):
                wr = w if r == 0 else pltpu.roll(w, ext - r, 0)
                for off in range(first_off, first_off + CONV_WIDTH):
                    if off % SUBLANES == r:
                        tap = wr[off - r:off - r + CONV_RB] * cw_ref[pl.ds(off - first_off, 1), lanes]
                        acc = tap if acc is None else acc + tap
            blocks.append(acc)
        y = jnp.concatenate(blocks, axis=1) + cb_ref[...]
        y = _layer_norm(y, lg_ref[...], lb_ref[...])
        cv_ref[pl.ds(r0, CONV_RB), :] = _silu(y).astype(cv_ref.dtype)


def _mix_in(x, w, conv_w, conv_b, ln_g, ln_b, *, t, tiles_per_seq):
    m, d = x.shape
    ch = conv_w.shape[1]
    assert m % t == 0 and t % CONV_RB == 0 and ch % LANES == 0
    assert QKVG_COLS % PROJ_NB == 0 and t % GLU_ROWS == 0
    assert w.shape[1] % LANES == 0 and w.shape[1] >= GLU_START + 2 * ch
    tile = lambda i: (i, 0)
    const = lambda i: (0, 0)
    return pl.pallas_call(
        functools.partial(_mix_in_kernel, tiles_per_seq=tiles_per_seq),
        grid=(m // t,),
        in_specs=[
            pl.BlockSpec((t, d), tile),
            pl.BlockSpec(w.shape, const, pipeline_mode=pl.Buffered(1)),
            pl.BlockSpec((CONV_WIDTH, ch), const),
            pl.BlockSpec((1, ch), const),
            pl.BlockSpec((1, ch), const),
            pl.BlockSpec((1, ch), const),
        ],
        out_specs=[
            pl.BlockSpec((t, QKVG_COLS), tile),
            pl.BlockSpec((t, GATE_RANK), tile),
            pl.BlockSpec((t, ch), tile),
        ],
        out_shape=[
            jax.ShapeDtypeStruct((m, QKVG_COLS), BF16),
            jax.ShapeDtypeStruct((m, GATE_RANK), F32),
            jax.ShapeDtypeStruct((m, ch), BF16),
        ],
        scratch_shapes=[pltpu.VMEM((t, d), BF16), pltpu.VMEM((t + CONV_HALO, ch), F32)],
        compiler_params=_params(("arbitrary",)),
        name="mix_in",
    )(x, w, conv_w, conv_b, ln_g, ln_b)


def _split_bf16(x):
    hi = x.astype(BF16)
    lo = (x - hi.astype(F32)).astype(BF16)
    return hi, lo


def _mix_tail_kernel(qkvg_ref, alr_ref, wg_hi_ref, wg_lo_ref, bgate_ref, ng_ref, x_ref, cv_ref,
                     wa_ref, wb_ref, lg_ref, lb_ref, o_ref, state_ref, gla_ref, *, c):
    @pl.when(pl.program_id(1) == 0)
    def _():
        state_ref[...] = jnp.zeros_like(state_ref)

    t = qkvg_ref.shape[1]
    row = lax.broadcasted_iota(jnp.int32, (c, c), 0)
    col = lax.broadcasted_iota(jnp.int32, (c, c), 1)
    causal = row >= col
    tril = causal.astype(BF16)
    scale = GLA_DK ** -0.5

    for blk in range(t // c):
        rows = pl.ds(blk * c, c)
        a_hi, a_lo = _split_bf16(alr_ref[0, rows, :])
        z = (_dot(a_hi, wg_hi_ref[...]) + _dot(a_lo, wg_hi_ref[...]) + _dot(a_hi, wg_lo_ref[...])
             + bgate_ref[...])
        log_a = jax.nn.log_sigmoid(z) / GATE_TAU
        la_hi, la_lo = _split_bf16(log_a)
        cum = _dot(tril, la_hi) + _dot(tril, la_lo)
        c_mid = cum[c // 2 - 1:c // 2, :]
        c_last = cum[c - 1:c, :]
        e_q = jnp.exp(cum - c_mid)
        e_k = jnp.exp(c_mid - cum)
        e_in = jnp.exp(cum)
        e_out = jnp.exp(c_last - cum)
        e_last = jnp.exp(c_last)

        for h in range(GLA_HEADS):
            ks = slice(h * GLA_DK, (h + 1) * GLA_DK)
            q = qkvg_ref[0, rows, pl.ds(h * GLA_DK, GLA_DK)].astype(F32) * scale
            k = qkvg_ref[0, rows, pl.ds(GLA_QK + h * GLA_DK, GLA_DK)].astype(F32)
            v = qkvg_ref[0, rows, pl.ds(2 * GLA_QK + h * GLA_DV, GLA_DV)]
            g = qkvg_ref[0, rows, pl.ds(2 * GLA_QK + GLA_V + h * GLA_DV, GLA_DV)].astype(F32)

            q_intra = (q * e_q[:, ks]).astype(BF16)
            k_intra = (k * e_k[:, ks]).astype(BF16)
            q_inter = (q * e_in[:, ks]).astype(BF16)
            k_out = (k * e_out[:, ks]).astype(BF16)

            scores = jnp.where(causal, _dot_nt(q_intra, k_intra), 0.0).astype(BF16)
            state_t = state_ref[h]
            o = _dot(scores, v) + _dot_nt(q_inter, state_t.astype(BF16))
            state_ref[h] = state_t * e_last[:, ks] + _dot_tn(v, k_out)

            o = o * lax.rsqrt(jnp.mean(o * o, axis=-1, keepdims=True) + LN_EPS) * ng_ref[...]
            gla_ref[rows, pl.ds(h * GLA_DV, GLA_DV)] = (o * _silu(g)).astype(gla_ref.dtype)

    for r0 in range(0, t, EPILOGUE_ROWS):
        rows = pl.ds(r0, EPILOGUE_ROWS)
        mix = _dot(gla_ref[rows, :], wa_ref[...]) + _dot(cv_ref[0, rows, :], wb_ref[...])
        o_ref[0, rows, :] = _layer_norm(ALPHA * x_ref[0, rows, :] + mix, lg_ref[...], lb_ref[...])


def _mix_tail(qkvg, alr, w_gate, b_gate, norm_g, x, cv, w_out, ln_g, ln_b, *, t, c):
    bsz, s, d = x.shape
    assert s % t == 0 and t % c == 0 and t % EPILOGUE_ROWS == 0
    assert w_out.shape == (GLA_V + cv.shape[2], d) and cv.shape[2] == GLA_V
    wg_hi, wg_lo = _split_bf16(w_gate)
    tile = lambda b, i: (b, i, 0)
    const = lambda b, i: (0, 0)
    return pl.pallas_call(
        functools.partial(_mix_tail_kernel, c=c),
        grid=(bsz, s // t),
        in_specs=[
            pl.BlockSpec((1, t, QKVG_COLS), tile),
            pl.BlockSpec((1, t, GATE_RANK), tile),
            pl.BlockSpec((GATE_RANK, GLA_QK), const),
            pl.BlockSpec((GATE_RANK, GLA_QK), const),
            pl.BlockSpec((1, GLA_QK), const),
            pl.BlockSpec((1, GLA_DV), const),
            pl.BlockSpec((1, t, d), tile),
            pl.BlockSpec((1, t, GLA_V), tile),
            pl.BlockSpec((GLA_V, d), const, pipeline_mode=pl.Buffered(1)),
            pl.BlockSpec((GLA_V, d), lambda b, i: (1, 0), pipeline_mode=pl.Buffered(1)),
            pl.BlockSpec((1, d), const),
            pl.BlockSpec((1, d), const),
        ],
        out_specs=pl.BlockSpec((1, t, d), tile),
        out_shape=jax.ShapeDtypeStruct((bsz, s, d), F32),
        scratch_shapes=[pltpu.VMEM((GLA_HEADS, GLA_DV, GLA_DK), F32), pltpu.VMEM((t, GLA_V), BF16)],
        compiler_params=_params(("parallel", "arbitrary")),
        name="mix_tail",
    )(qkvg, alr, wg_hi, wg_lo, b_gate, norm_g, x, cv, w_out, w_out, ln_g, ln_b)


def _ple_kernel(x_ref, p_ref, wg_ref, bg_ref, wp_ref, g_ref, b_ref, o_ref):
    for r0 in range(0, x_ref.shape[0], EPILOGUE_ROWS):
        rows = pl.ds(r0, EPILOGUE_ROWS)
        x = x_ref[rows, :]
        gate = jax.nn.sigmoid(_dot(x.astype(BF16), wg_ref[...]) + bg_ref[...])
        emb = _dot(p_ref[rows, :].astype(BF16), wp_ref[...])
        o_ref[rows, :] = _layer_norm(ALPHA * x + gate * emb, g_ref[...], b_ref[...])


def _ple(x, p, w_gate, b_gate, w_proj, g, b, *, tm):
    m, d = x.shape
    dp = p.shape[1]
    assert m % tm == 0
    return pl.pallas_call(
        _ple_kernel,
        grid=(m // tm,),
        in_specs=[
            pl.BlockSpec((tm, d), lambda i: (i, 0)),
            pl.BlockSpec((tm, dp), lambda i: (i, 0)),
            pl.BlockSpec((d, d), lambda i: (0, 0), pipeline_mode=pl.Buffered(1)),
            pl.BlockSpec((1, d), lambda i: (0, 0)),
            pl.BlockSpec((dp, d), lambda i: (0, 0), pipeline_mode=pl.Buffered(1)),
            pl.BlockSpec((1, d), lambda i: (0, 0)),
            pl.BlockSpec((1, d), lambda i: (0, 0)),
        ],
        out_specs=pl.BlockSpec((tm, d), lambda i: (i, 0)),
        out_shape=jax.ShapeDtypeStruct((m, d), F32),
        compiler_params=_params(("parallel",)),
        name="ple",
    )(x, p, w_gate, b_gate, w_proj, g, b)


def _layer(x, p, ffn1_w_in, ffn1_w_out, ln_ffn1_g, ln_ffn1_b, w_mix_in, gla_w_gate,
           gla_b_gate, gla_norm_g, conv_w, conv_b, conv_ln_g, conv_ln_b, w_mix_out,
           ln_mix_g, ln_mix_b, ffn2_w_in, ffn2_w_out, ln_ffn2_g, ln_ffn2_b,
           ple_w_gate, ple_b_gate, ple_w_proj, ln_ple_g, ln_ple_b, *, tiles):
    bsz, s, d = x.shape
    m = bsz * s
    row = lambda a: a.reshape(1, -1)
    x2 = x.reshape(m, d)

    x2 = _ffn(x2, *_ffn_weights(ffn1_w_in, ffn1_w_out, tiles["ffn_tf"]), row(ln_ffn1_g), row(ln_ffn1_b),
              tm=tiles["ffn_tm"])

    w_in = jnp.pad(w_mix_in.astype(BF16), ((0, 0), (0, -w_mix_in.shape[1] % LANES)))
    assert s % tiles["mix_t"] == 0
    qkvg, alr, cv = _mix_in(x2, w_in, conv_w, row(conv_b), row(conv_ln_g), row(conv_ln_b),
                            t=tiles["mix_t"], tiles_per_seq=s // tiles["mix_t"])
    x2 = _mix_tail(qkvg.reshape(bsz, s, -1), alr.reshape(bsz, s, -1), gla_w_gate, row(gla_b_gate),
                   row(gla_norm_g), x2.reshape(bsz, s, d), cv.reshape(bsz, s, -1),
                   w_mix_out.astype(BF16), row(ln_mix_g), row(ln_mix_b),
                   t=tiles["gla_t"], c=tiles["gla_c"]).reshape(m, d)

    x2 = _ffn(x2, *_ffn_weights(ffn2_w_in, ffn2_w_out, tiles["ffn_tf"]), row(ln_ffn2_g), row(ln_ffn2_b),
              tm=tiles["ffn_tm"])

    x2 = _ple(x2, p.reshape(m, -1), ple_w_gate.astype(BF16), row(ple_b_gate),
              ple_w_proj.astype(BF16), row(ln_ple_g), row(ln_ple_b), tm=tiles["out_tm"])
    return x2.reshape(bsz, s, d)


TILES = dict(ffn_tm=1024, ffn_tf=512, mix_t=512, gla_t=512, gla_c=128, out_tm=1024)


def kernel(x, p, ffn1_w_in, ffn1_w_out, ln_ffn1_g, ln_ffn1_b, w_mix_in, gla_w_gate, gla_b_gate, gla_norm_g, conv_w, conv_b, conv_ln_g, conv_ln_b, w_mix_out, ln_mix_g, ln_mix_b, ffn2_w_in, ffn2_w_out, ln_ffn2_g, ln_ffn2_b, ple_w_gate, ple_b_gate, ple_w_proj, ln_ple_g, ln_ple_b):
    assert x.shape[0] == p.shape[1] and ffn1_w_in.shape[0] == DEPTH
    return _layer(x, p[0], ffn1_w_in[0], ffn1_w_out[0], ln_ffn1_g[0], ln_ffn1_b[0], w_mix_in[0],
                  gla_w_gate[0], gla_b_gate[0], gla_norm_g[0], conv_w[0], conv_b[0], conv_ln_g[0],
                  conv_ln_b[0], w_mix_out[0], ln_mix_g[0], ln_mix_b[0], ffn2_w_in[0], ffn2_w_out[0],
                  ln_ffn2_g[0], ln_ffn2_b[0], ple_w_gate[0], ple_b_gate[0], ple_w_proj[0],
                  ln_ple_g[0], ln_ple_b[0], tiles=TILES)
```

```python
import functools

import jax
import jax.numpy as jnp
from jax import lax
from jax.experimental import pallas as pl
from jax.experimental.pallas import tpu as pltpu

DEPTH = 1
GLA_HEADS = 4
GLA_DK = 128
GLA_DV = 256
GLA_QK = GLA_HEADS * GLA_DK
GLA_V = GLA_HEADS * GLA_DV
CONV_CH = 1024
GATE_RANK = 16
GATE_TAU = 16.0
CONV_WIDTH = 31
LN_EPS = 1e-5
ALPHA = (2.0 * DEPTH) ** 0.25

QKVG_COLS = 2 * GLA_QK + 2 * GLA_V
GLU_START = QKVG_COLS + GATE_RANK

SUBLANES = 8
LANES = 128
CONV_HALO = 32
EPILOGUE_ROWS = 256

BF16 = jnp.bfloat16
F32 = jnp.float32

VMEM_LIMIT = 56 * 1024 * 1024
FFN_VMEM_LIMIT = 62 * 1024 * 1024


def _dot(a, b):
    return jnp.dot(a, b, preferred_element_type=F32)


def _dot_nt(a, b):
    return lax.dot_general(a, b, (((1,), (1,)), ((), ())), preferred_element_type=F32)


def _dot_tn(a, b):
    return lax.dot_general(a, b, (((0,), (0,)), ((), ())), preferred_element_type=F32)


def _layer_norm(y, g, b):
    mu = jnp.mean(y, axis=-1, keepdims=True)
    d = y - mu
    var = jnp.mean(d * d, axis=-1, keepdims=True)
    return d * lax.rsqrt(var + LN_EPS) * g + b


def _silu(x):
    return x * jax.nn.sigmoid(x)


def _params(sem, vmem_limit=VMEM_LIMIT):
    return pltpu.CompilerParams(dimension_semantics=sem, vmem_limit_bytes=vmem_limit)


def _ffn_kernel(x_ref, wgu_ref, wo_ref, g_ref, b_ref, o_ref, xb_ref, act_ref, *, n_ff):
    s = pl.program_id(0)
    n_steps = pl.num_programs(0) - 1
    j_prev = (s + n_ff - 1) % n_ff

    def stage_a(cast_x):
        if cast_x:
            xb_ref[...] = x_ref[...].astype(BF16)
        tf = act_ref.shape[2]
        gate_up = _dot(xb_ref[...], wgu_ref[0])
        act_ref[s % 2] = (0.5 * _silu(gate_up[:, :tf]) * gate_up[:, tf:]).astype(BF16)

    def stage_b(first, last):
        chunk = EPILOGUE_ROWS if last else o_ref.shape[0]
        wo = wo_ref[...].astype(BF16)
        for r0 in range(0, o_ref.shape[0], chunk):
            rows = pl.ds(r0, chunk)
            part = _dot(act_ref[(s + 1) % 2, rows, :], wo)
            if first:
                acc = ALPHA * x_ref[rows, :] + part
            else:
                acc = o_ref[rows, :] + part
            if last:
                acc = _layer_norm(acc, g_ref[...], b_ref[...])
            o_ref[rows, :] = acc

    first_b = j_prev == 0
    last_b = j_prev == n_ff - 1

    @pl.when(s == 0)
    def _():
        stage_a(True)

    @pl.when(jnp.logical_and(s > 0, first_b))
    def _():
        stage_b(True, False)
        stage_a(False)

    @pl.when(jnp.logical_and(jnp.logical_not(first_b), jnp.logical_not(last_b)))
    def _():
        stage_b(False, False)
        stage_a(False)

    @pl.when(jnp.logical_and(last_b, jnp.logical_and(s > 0, s < n_steps)))
    def _():
        stage_b(False, True)
        stage_a(True)

    @pl.when(s == n_steps)
    def _():
        stage_b(False, True)


def _pack_kernel(gate_ref, up_ref, o_ref):
    tf = gate_ref.shape[1]
    o_ref[0, :, pl.ds(0, tf)] = gate_ref[...].astype(BF16)
    o_ref[0, :, pl.ds(tf, tf)] = up_ref[...].astype(BF16)


def _ffn_weights(w_in, w_out, tf):
    d, two_ff = w_in.shape
    d_ff = two_ff // 2
    n_ff = d_ff // tf
    assert d_ff % tf == 0
    w_gu = pl.pallas_call(
        _pack_kernel,
        grid=(n_ff,),
        in_specs=[
            pl.BlockSpec((d, tf), lambda j: (0, j)),
            pl.BlockSpec((d, tf), lambda j: (0, j + n_ff)),
        ],
        out_specs=pl.BlockSpec((1, d, 2 * tf), lambda j: (j, 0, 0)),
        out_shape=jax.ShapeDtypeStruct((n_ff, d, 2 * tf), BF16),
        compiler_params=_params(("parallel",)),
        name="pack_w_in",
    )(w_in, w_in)
    return w_gu, w_out


def _ffn(x, w_gu, w_out, g, b, *, tm):
    m, d = x.shape
    n_ff, _, two_tf = w_gu.shape
    tf = two_tf // 2
    assert m % tm == 0 and w_out.shape[0] == n_ff * tf and n_ff >= 2
    n_steps = (m // tm) * n_ff
    cur = lambda s: jnp.minimum(s, n_steps - 1)
    prev = lambda s: jnp.maximum(s - 1, 0)
    return pl.pallas_call(
        functools.partial(_ffn_kernel, n_ff=n_ff),
        grid=(n_steps + 1,),
        in_specs=[
            pl.BlockSpec((tm, d), lambda s: (cur(s) // n_ff, 0)),
            pl.BlockSpec((1, d, two_tf), lambda s: (cur(s) % n_ff, 0, 0)),
            pl.BlockSpec((tf, d), lambda s: (prev(s) % n_ff, 0)),
            pl.BlockSpec((1, d), lambda s: (0, 0)),
            pl.BlockSpec((1, d), lambda s: (0, 0)),
        ],
        out_specs=pl.BlockSpec((tm, d), lambda s: (prev(s) // n_ff, 0)),
        out_shape=jax.ShapeDtypeStruct((m, d), F32),
        scratch_shapes=[pltpu.VMEM((tm, d), BF16), pltpu.VMEM((2, tm, tf), BF16)],
        compiler_params=_params(("arbitrary",), FFN_VMEM_LIMIT),
        name="ffn",
    )(x, w_gu, w_out, g, b)


CONV_RB = 64
PROJ_NB = 768
GLU_ROWS = 256


def _mix_in_kernel(x_ref, w_ref, cw_ref, cb_ref, lg_ref, lb_ref, qkvg_ref, alr_ref, cv_ref,
                   xb_ref, win_ref, *, tiles_per_seq):
    i = pl.program_id(0)
    t = x_ref.shape[0]
    ch = cv_ref.shape[1]

    @pl.when(i == 0)
    def _():
        win_ref[pl.ds(t, CONV_HALO), :] = jnp.zeros((CONV_HALO, ch), F32)

    tail = win_ref[pl.ds(t, CONV_HALO), :]
    win_ref[pl.ds(0, CONV_HALO), :] = jnp.where(i % tiles_per_seq == 0, 0.0, tail)

    xb_ref[...] = x_ref[...].astype(BF16)
    for c0 in range(0, QKVG_COLS, PROJ_NB):
        cols = pl.ds(c0, PROJ_NB)
        qkvg_ref[:, cols] = _dot(xb_ref[...], w_ref[:, cols]).astype(BF16)
    n_rest = w_ref.shape[1] - QKVG_COLS
    for r0 in range(0, t, GLU_ROWS):
        rows = pl.ds(r0, GLU_ROWS)
        rest = _dot(xb_ref[rows, :], w_ref[:, pl.ds(QKVG_COLS, n_rest)])
        alr_ref[rows, :] = rest[:, :GATE_RANK]
        a = rest[:, GATE_RANK:GATE_RANK + ch]
        gate = rest[:, GATE_RANK + ch:GATE_RANK + 2 * ch]
        win_ref[pl.ds(CONV_HALO + r0, GLU_ROWS), :] = a * jax.nn.sigmoid(gate)

    first_off = CONV_HALO - (CONV_WIDTH - 1)
    ext = CONV_RB + CONV_HALO
    for r0 in range(0, t, CONV_RB):
        blocks = []
        for l0 in range(0, ch, LANES):
            lanes = pl.ds(l0, LANES)
            w = win_ref[pl.ds(r0, ext), lanes]
            acc = None
            for r in range(SUBLANES):
                wr = w if r == 0 else pltpu.roll(w, ext - r, 0)
                for off in range(first_off, first_off + CONV_WIDTH):
                    if off % SUBLANES == r:
                        tap = wr[off - r:off - r + CONV_RB] * cw_ref[pl.ds(off - first_off, 1), lanes]
                        acc = tap if acc is None else acc + tap
            blocks.append(acc)
        y = jnp.concatenate(blocks, axis=1) + cb_ref[...]
        y = _layer_norm(y, lg_ref[...], lb_ref[...])
        cv_ref[pl.ds(r0, CONV_RB), :] = _silu(y).astype(cv_ref.dtype)


def _mix_in(x, w, conv_w, conv_b, ln_g, ln_b, *, t, tiles_per_seq):
    m, d = x.shape
    ch = conv_w.shape[1]
    assert m % t == 0 and t % CONV_RB == 0 and ch % LANES == 0
    assert QKVG_COLS % PROJ_NB == 0 and t % GLU_ROWS == 0
    assert w.shape[1] % LANES == 0 and w.shape[1] >= GLU_START + 2 * ch
    tile = lambda i: (i, 0)
    const = lambda i: (0, 0)
    return pl.pallas_call(
        functools.partial(_mix_in_kernel, tiles_per_seq=tiles_per_seq),
        grid=(m // t,),
        in_specs=[
            pl.BlockSpec((t, d), tile),
            pl.BlockSpec(w.shape, const, pipeline_mode=pl.Buffered(1)),
            pl.BlockSpec((CONV_WIDTH, ch), const),
            pl.BlockSpec((1, ch), const),
            pl.BlockSpec((1, ch), const),
            pl.BlockSpec((1, ch), const),
        ],
        out_specs=[
            pl.BlockSpec((t, QKVG_COLS), tile),
            pl.BlockSpec((t, GATE_RANK), tile),
            pl.BlockSpec((t, ch), tile),
        ],
        out_shape=[
            jax.ShapeDtypeStruct((m, QKVG_COLS), BF16),
            jax.ShapeDtypeStruct((m, GATE_RANK), F32),
            jax.ShapeDtypeStruct((m, ch), BF16),
        ],
        scratch_shapes=[pltpu.VMEM((t, d), BF16), pltpu.VMEM((t + CONV_HALO, ch), F32)],
        compiler_params=_params(("arbitrary",)),
        name="mix_in",
    )(x, w, conv_w, conv_b, ln_g, ln_b)


def _split_bf16(x):
    hi = x.astype(BF16)
    lo = (x - hi.astype(F32)).astype(BF16)
    return hi, lo


def _mix_tail_kernel(qkvg_ref, alr_ref, wg_hi_ref, wg_lo_ref, bgate_ref, ng_ref, x_ref, cv_ref,
                     wa_ref, wb_ref, lg_ref, lb_ref, o_ref, state_ref, gla_ref, *, c):
    @pl.when(pl.program_id(1) == 0)
    def _():
        state_ref[...] = jnp.zeros_like(state_ref)

    t = qkvg_ref.shape[1]
    row = lax.broadcasted_iota(jnp.int32, (c, c), 0)
    col = lax.broadcasted_iota(jnp.int32, (c, c), 1)
    causal = row >= col
    tril = causal.astype(BF16)
    scale = GLA_DK ** -0.5

    for blk in range(t // c):
        rows = pl.ds(blk * c, c)
        a_hi, a_lo = _split_bf16(alr_ref[0, rows, :])
        z = (_dot(a_hi, wg_hi_ref[...]) + _dot(a_lo, wg_hi_ref[...]) + _dot(a_hi, wg_lo_ref[...])
             + bgate_ref[...])
        log_a = jax.nn.log_sigmoid(z) / GATE_TAU
        la_hi, la_lo = _split_bf16(log_a)
        cum = _dot(tril, la_hi) + _dot(tril, la_lo)
        c_mid = cum[c // 2 - 1:c // 2, :]
        c_last = cum[c - 1:c, :]
        e_q = jnp.exp(cum - c_mid)
        e_k = jnp.exp(c_mid - cum)
        e_in = jnp.exp(cum)
        e_out = jnp.exp(c_last - cum)
        e_last = jnp.exp(c_last)

        for h in range(GLA_HEADS):
            ks = slice(h * GLA_DK, (h + 1) * GLA_DK)
            q = qkvg_ref[0, rows, pl.ds(h * GLA_DK, GLA_DK)].astype(F32) * scale
            k = qkvg_ref[0, rows, pl.ds(GLA_QK + h * GLA_DK, GLA_DK)].astype(F32)
            v = qkvg_ref[0, rows, pl.ds(2 * GLA_QK + h * GLA_DV, GLA_DV)]
            g = qkvg_ref[0, rows, pl.ds(2 * GLA_QK + GLA_V + h * GLA_DV, GLA_DV)].astype(F32)

            q_intra = (q * e_q[:, ks]).astype(BF16)
            k_intra = (k * e_k[:, ks]).astype(BF16)
            q_inter = (q * e_in[:, ks]).astype(BF16)
            k_out = (k * e_out[:, ks]).astype(BF16)

            scores = jnp.where(causal, _dot_nt(q_intra, k_intra), 0.0).astype(BF16)
            state_t = state_ref[h]
            o = _dot(scores, v) + _dot_nt(q_inter, state_t.astype(BF16))
            state_ref[h] = state_t * e_last[:, ks] + _dot_tn(v, k_out)

            o = o * lax.rsqrt(jnp.mean(o * o, axis=-1, keepdims=True) + LN_EPS) * ng_ref[...]
            gla_ref[rows, pl.ds(h * GLA_DV, GLA_DV)] = (o * _silu(g)).astype(gla_ref.dtype)

    for r0 in range(0, t, EPILOGUE_ROWS):
        rows = pl.ds(r0, EPILOGUE_ROWS)
        mix = _dot(gla_ref[rows, :], wa_ref[...]) + _dot(cv_ref[0, rows, :], wb_ref[...])
        o_ref[0, rows, :] = _layer_norm(ALPHA * x_ref[0, rows, :] + mix, lg_ref[...], lb_ref[...])


def _mix_tail(qkvg, alr, w_gate, b_gate, norm_g, x, cv, w_out, ln_g, ln_b, *, t, c):
    bsz, s, d = x.shape
    assert s % t == 0 and t % c == 0 and t % EPILOGUE_ROWS == 0
    assert w_out.shape == (GLA_V + cv.shape[2], d) and cv.shape[2] == GLA_V
    wg_hi, wg_lo = _split_bf16(w_gate)
    tile = lambda b, i: (b, i, 0)
    const = lambda b, i: (0, 0)
    return pl.pallas_call(
        functools.partial(_mix_tail_kernel, c=c),
        grid=(bsz, s // t),
        in_specs=[
            pl.BlockSpec((1, t, QKVG_COLS), tile),
            pl.BlockSpec((1, t, GATE_RANK), tile),
            pl.BlockSpec((GATE_RANK, GLA_QK), const),
            pl.BlockSpec((GATE_RANK, GLA_QK), const),
            pl.BlockSpec((1, GLA_QK), const),
            pl.BlockSpec((1, GLA_DV), const),
            pl.BlockSpec((1, t, d), tile),
            pl.BlockSpec((1, t, GLA_V), tile),
            pl.BlockSpec((GLA_V, d), const, pipeline_mode=pl.Buffered(1)),
            pl.BlockSpec((GLA_V, d), lambda b, i: (1, 0), pipeline_mode=pl.Buffered(1)),
            pl.BlockSpec((1, d), const),
            pl.BlockSpec((1, d), const),
        ],
        out_specs=pl.BlockSpec((1, t, d), tile),
        out_shape=jax.ShapeDtypeStruct((bsz, s, d), F32),
        scratch_shapes=[pltpu.VMEM((GLA_HEADS, GLA_DV, GLA_DK), F32), pltpu.VMEM((t, GLA_V), BF16)],
        compiler_params=_params(("parallel", "arbitrary")),
        name="mix_tail",
    )(qkvg, alr, wg_hi, wg_lo, b_gate, norm_g, x, cv, w_out, w_out, ln_g, ln_b)


def _ple_kernel(x_ref, p_ref, wg_ref, bg_ref, wp_ref, g_ref, b_ref, o_ref):
    for r0 in range(0, x_ref.shape[0], EPILOGUE_ROWS):
        rows = pl.ds(r0, EPILOGUE_ROWS)
        x = x_ref[rows, :]
        gate = jax.nn.sigmoid(_dot(x.astype(BF16), wg_ref[...]) + bg_ref[...])
        emb = _dot(p_ref[rows, :].astype(BF16), wp_ref[...])
        o_ref[rows, :] = _layer_norm(ALPHA * x + gate * emb, g_ref[...], b_ref[...])


def _ple(x, p, w_gate, b_gate, w_proj, g, b, *, tm):
    m, d = x.shape
    dp = p.shape[1]
    assert m % tm == 0
    return pl.pallas_call(
        _ple_kernel,
        grid=(m // tm,),
        in_specs=[
            pl.BlockSpec((tm, d), lambda i: (i, 0)),
            pl.BlockSpec((tm, dp), lambda i: (i, 0)),
            pl.BlockSpec((d, d), lambda i: (0, 0), pipeline_mode=pl.Buffered(1)),
            pl.BlockSpec((1, d), lambda i: (0, 0)),
            pl.BlockSpec((dp, d), lambda i: (0, 0), pipeline_mode=pl.Buffered(1)),
            pl.BlockSpec((1, d), lambda i: (0, 0)),
            pl.BlockSpec((1, d), lambda i: (0, 0)),
        ],
        out_specs=pl.BlockSpec((tm, d), lambda i: (i, 0)),
        out_shape=jax.ShapeDtypeStruct((m, d), F32),
        compiler_params=_params(("parallel",)),
        name="ple",
    )(x, p, w_gate, b_gate, w_proj, g, b)


def _layer(x, p, ffn1_w_in, ffn1_w_out, ln_ffn1_g, ln_ffn1_b, w_mix_in, gla_w_gate,
           gla_b_gate, gla_norm_g, conv_w, conv_b, conv_ln_g, conv_ln_b, w_mix_out,
           ln_mix_g, ln_mix_b, ffn2_w_in, ffn2_w_out, ln_ffn2_g, ln_ffn2_b,
           ple_w_gate, ple_b_gate, ple_w_proj, ln_ple_g, ln_ple_b, *, tiles):
    bsz, s, d = x.shape
    m = bsz * s
    row = lambda a: a.reshape(1, -1)
    x2 = x.reshape(m, d)

    x2 = _ffn(x2, *_ffn_weights(ffn1_w_in, ffn1_w_out, tiles["ffn_tf"]), row(ln_ffn1_g), row(ln_ffn1_b),
              tm=tiles["ffn_tm"])

    w_in = jnp.pad(w_mix_in.astype(BF16), ((0, 0), (0, -w_mix_in.shape[1] % LANES)))
    assert s % tiles["mix_t"] == 0
    qkvg, alr, cv = _mix_in(x2, w_in, conv_w, row(conv_b), row(conv_ln_g), row(conv_ln_b),
                            t=tiles["mix_t"], tiles_per_seq=s // tiles["mix_t"])
    x2 = _mix_tail(qkvg.reshape(bsz, s, -1), alr.reshape(bsz, s, -1), gla_w_gate, row(gla_b_gate),
                   row(gla_norm_g), x2.reshape(bsz, s, d), cv.reshape(bsz, s, -1),
                   w_mix_out.astype(BF16), row(ln_mix_g), row(ln_mix_b),
                   t=tiles["gla_t"], c=tiles["gla_c"]).reshape(m, d)

    x2 = _ffn(x2, *_ffn_weights(ffn2_w_in, ffn2_w_out, tiles["ffn_tf"]), row(ln_ffn2_g), row(ln_ffn2_b),
              tm=tiles["ffn_tm"])

    x2 = _ple(x2, p.reshape(m, -1), ple_w_gate.astype(BF16), row(ple_b_gate),
              ple_w_proj.astype(BF16), row(ln_ple_g), row(ln_ple_b), tm=tiles["out_tm"])
    return x2.reshape(bsz, s, d)


TILES = dict(ffn_tm=1024, ffn_tf=512, mix_t=512, gla_t=512, gla_c=256, out_tm=1024)


def kernel(x, p, ffn1_w_in, ffn1_w_out, ln_ffn1_g, ln_ffn1_b, w_mix_in, gla_w_gate, gla_b_gate, gla_norm_g, conv_w, conv_b, conv_ln_g, conv_ln_b, w_mix_out, ln_mix_g, ln_mix_b, ffn2_w_in, ffn2_w_out, ln_ffn2_g, ln_ffn2_b, ple_w_gate, ple_b_gate, ple_w_proj, ln_ple_g, ln_ple_b):
    assert x.shape[0] == p.shape[1] and ffn1_w_in.shape[0] == DEPTH
    return _layer(x, p[0], ffn1_w_in[0], ffn1_w_out[0], ln_ffn1_g[0], ln_ffn1_b[0], w_mix_in[0],
                  gla_w_gate[0], gla_b_gate[0], gla_norm_g[0], conv_w[0], conv_b[0], conv_ln_g[0],
                  conv_ln_b[0], w_mix_out[0], ln_mix_g[0], ln_mix_b[0], ffn2_w_in[0], ffn2_w_out[0],
                  ln_ffn2_g[0], ln_ffn2_b[0], ple_w_gate[0], ple_b_gate[0], ple_w_proj[0],
                  ln_ple_g[0], ln_ple_b[0], tiles=TILES)
```

```python
import functools

import jax
import jax.numpy as jnp
from jax import lax
from jax.experimental import pallas as pl
from jax.experimental.pallas import tpu as pltpu

DEPTH = 1
GLA_HEADS = 4
GLA_DK = 128
GLA_DV = 256
GLA_QK = GLA_HEADS * GLA_DK
GLA_V = GLA_HEADS * GLA_DV
CONV_CH = 1024
GATE_RANK = 16
GATE_TAU = 16.0
CONV_WIDTH = 31
LN_EPS = 1e-5
ALPHA = (2.0 * DEPTH) ** 0.25

QKVG_COLS = 2 * GLA_QK + 2 * GLA_V
GLU_START = QKVG_COLS + GATE_RANK

SUBLANES = 8
LANES = 128
CONV_HALO = 32
EPILOGUE_ROWS = 256

BF16 = jnp.bfloat16
F32 = jnp.float32

VMEM_LIMIT = 56 * 1024 * 1024
FFN_VMEM_LIMIT = 62 * 1024 * 1024


def _dot(a, b):
    return jnp.dot(a, b, preferred_element_type=F32)


def _dot_nt(a, b):
    return lax.dot_general(a, b, (((1,), (1,)), ((), ())), preferred_element_type=F32)


def _dot_tn(a, b):
    return lax.dot_general(a, b, (((0,), (0,)), ((), ())), preferred_element_type=F32)


def _layer_norm(y, g, b):
    mu = jnp.mean(y, axis=-1, keepdims=True)
    d = y - mu
    var = jnp.mean(d * d, axis=-1, keepdims=True)
    return d * lax.rsqrt(var + LN_EPS) * g + b


def _silu(x):
    return x * jax.nn.sigmoid(x)


def _params(sem, vmem_limit=VMEM_LIMIT):
    return pltpu.CompilerParams(dimension_semantics=sem, vmem_limit_bytes=vmem_limit)


def _ffn_kernel(x_ref, wgu_ref, wo_ref, g_ref, b_ref, o_ref, xb_ref, act_ref, *, n_ff):
    s = pl.program_id(0)
    n_steps = pl.num_programs(0) - 1
    j_prev = (s + n_ff - 1) % n_ff

    def stage_a(cast_x):
        if cast_x:
            xb_ref[...] = x_ref[...].astype(BF16)
        tf = act_ref.shape[2]
        gate_up = _dot(xb_ref[...], wgu_ref[0])
        act_ref[s % 2] = (0.5 * _silu(gate_up[:, :tf]) * gate_up[:, tf:]).astype(BF16)

    def stage_b(first, last):
        chunk = EPILOGUE_ROWS if last else o_ref.shape[0]
        wo = wo_ref[...].astype(BF16)
        for r0 in range(0, o_ref.shape[0], chunk):
            rows = pl.ds(r0, chunk)
            part = _dot(act_ref[(s + 1) % 2, rows, :], wo)
            if first:
                acc = ALPHA * x_ref[rows, :] + part
            else:
                acc = o_ref[rows, :] + part
            if last:
                acc = _layer_norm(acc, g_ref[...], b_ref[...])
            o_ref[rows, :] = acc

    first_b = j_prev == 0
    last_b = j_prev == n_ff - 1

    @pl.when(s == 0)
    def _():
        stage_a(True)

    @pl.when(jnp.logical_and(s > 0, first_b))
    def _():
        stage_b(True, False)
        stage_a(False)

    @pl.when(jnp.logical_and(jnp.logical_not(first_b), jnp.logical_not(last_b)))
    def _():
        stage_b(False, False)
        stage_a(False)

    @pl.when(jnp.logical_and(last_b, jnp.logical_and(s > 0, s < n_steps)))
    def _():
        stage_b(False, True)
        stage_a(True)

    @pl.when(s == n_steps)
    def _():
        stage_b(False, True)


def _pack_kernel(gate_ref, up_ref, o_ref):
    tf = gate_ref.shape[1]
    o_ref[0, :, pl.ds(0, tf)] = gate_ref[...].astype(BF16)
    o_ref[0, :, pl.ds(tf, tf)] = up_ref[...].astype(BF16)


def _pack_w_in(w_in, tf):
    d, two_ff = w_in.shape
    d_ff = two_ff // 2
    n_ff = d_ff // tf
    assert d_ff % tf == 0
    return pl.pallas_call(
        _pack_kernel,
        grid=(n_ff,),
        in_specs=[
            pl.BlockSpec((d, tf), lambda j: (0, j)),
            pl.BlockSpec((d, tf), lambda j: (0, j + n_ff)),
        ],
        out_specs=pl.BlockSpec((1, d, 2 * tf), lambda j: (j, 0, 0)),
        out_shape=jax.ShapeDtypeStruct((n_ff, d, 2 * tf), BF16),
        compiler_params=_params(("parallel",)),
        name="pack_w_in",
    )(w_in, w_in)


def _ffn(x, w_gu, w_out, g, b, *, tm):
    m, d = x.shape
    n_ff, _, two_tf = w_gu.shape
    tf = two_tf // 2
    assert m % tm == 0 and w_out.shape[0] == n_ff * tf and n_ff >= 2
    n_steps = (m // tm) * n_ff
    cur = lambda s: jnp.minimum(s, n_steps - 1)
    prev = lambda s: jnp.maximum(s - 1, 0)
    return pl.pallas_call(
        functools.partial(_ffn_kernel, n_ff=n_ff),
        grid=(n_steps + 1,),
        in_specs=[
            pl.BlockSpec((tm, d), lambda s: (cur(s) // n_ff, 0)),
            pl.BlockSpec((1, d, two_tf), lambda s: (cur(s) % n_ff, 0, 0)),
            pl.BlockSpec((tf, d), lambda s: (prev(s) % n_ff, 0)),
            pl.BlockSpec((1, d), lambda s: (0, 0)),
            pl.BlockSpec((1, d), lambda s: (0, 0)),
        ],
        out_specs=pl.BlockSpec((tm, d), lambda s: (prev(s) // n_ff, 0)),
        out_shape=jax.ShapeDtypeStruct((m, d), F32),
        scratch_shapes=[pltpu.VMEM((tm, d), BF16), pltpu.VMEM((2, tm, tf), BF16)],
        compiler_params=_params(("arbitrary",), FFN_VMEM_LIMIT),
        name="ffn",
    )(x, w_gu, w_out, g, b)


CONV_RB = 64
PROJ_NB = 768
GLU_ROWS = 256


def _mix_in_kernel(x_ref, w_ref, cw_ref, cb_ref, lg_ref, lb_ref, qkvg_ref, alr_ref, cv_ref,
                   xb_ref, win_ref, *, tiles_per_seq):
    i = pl.program_id(0)
    t = x_ref.shape[0]
    ch = cv_ref.shape[1]

    @pl.when(i == 0)
    def _():
        win_ref[pl.ds(t, CONV_HALO), :] = jnp.zeros((CONV_HALO, ch), F32)

    tail = win_ref[pl.ds(t, CONV_HALO), :]
    win_ref[pl.ds(0, CONV_HALO), :] = jnp.where(i % tiles_per_seq == 0, 0.0, tail)

    xb_ref[...] = x_ref[...].astype(BF16)
    for c0 in range(0, QKVG_COLS, PROJ_NB):
        cols = pl.ds(c0, PROJ_NB)
        qkvg_ref[:, cols] = _dot(xb_ref[...], w_ref[:, cols]).astype(BF16)
    n_rest = w_ref.shape[1] - QKVG_COLS
    for r0 in range(0, t, GLU_ROWS):
        rows = pl.ds(r0, GLU_ROWS)
        rest = _dot(xb_ref[rows, :], w_ref[:, pl.ds(QKVG_COLS, n_rest)])
        alr_ref[rows, :] = rest[:, :GATE_RANK]
        a = rest[:, GATE_RANK:GATE_RANK + ch]
        gate = rest[:, GATE_RANK + ch:GATE_RANK + 2 * ch]
        win_ref[pl.ds(CONV_HALO + r0, GLU_ROWS), :] = a * jax.nn.sigmoid(gate)

    first_off = CONV_HALO - (CONV_WIDTH - 1)
    ext = CONV_RB + CONV_HALO
    for r0 in range(0, t, CONV_RB):
        blocks = []
        for l0 in range(0, ch, LANES):
            lanes = pl.ds(l0, LANES)
            w = win_ref[pl.ds(r0, ext), lanes]
            acc = None
            for r in range(SUBLANES):
                wr = w if r == 0 else pltpu.roll(w, ext - r, 0)
                for off in range(first_off, first_off + CONV_WIDTH):
                    if off % SUBLANES == r:
                        tap = wr[off - r:off - r + CONV_RB] * cw_ref[pl.ds(off - first_off, 1), lanes]
                        acc = tap if acc is None else acc + tap
            blocks.append(acc)
        y = jnp.concatenate(blocks, axis=1) + cb_ref[...]
        y = _layer_norm(y, lg_ref[...], lb_ref[...])
        cv_ref[pl.ds(r0, CONV_RB), :] = _silu(y).astype(cv_ref.dtype)


def _mix_in(x, w, conv_w, conv_b, ln_g, ln_b, *, t, tiles_per_seq):
    m, d = x.shape
    ch = conv_w.shape[1]
    assert m % t == 0 and t % CONV_RB == 0 and ch % LANES == 0
    assert QKVG_COLS % PROJ_NB == 0 and t % GLU_ROWS == 0
    assert w.shape[1] % LANES == 0 and w.shape[1] >= GLU_START + 2 * ch
    tile = lambda i: (i, 0)
    const = lambda i: (0, 0)
    return pl.pallas_call(
        functools.partial(_mix_in_kernel, tiles_per_seq=tiles_per_seq),
        grid=(m // t,),
        in_specs=[
            pl.BlockSpec((t, d), tile),
            pl.BlockSpec(w.shape, const, pipeline_mode=pl.Buffered(1)),
            pl.BlockSpec((CONV_WIDTH, ch), const),
            pl.BlockSpec((1, ch), const),
            pl.BlockSpec((1, ch), const),
            pl.BlockSpec((1, ch), const),
        ],
        out_specs=[
            pl.BlockSpec((t, QKVG_COLS), tile),
            pl.BlockSpec((t, GATE_RANK), tile),
            pl.BlockSpec((t, ch), tile),
        ],
        out_shape=[
            jax.ShapeDtypeStruct((m, QKVG_COLS), BF16),
            jax.ShapeDtypeStruct((m, GATE_RANK), F32),
            jax.ShapeDtypeStruct((m, ch), BF16),
        ],
        scratch_shapes=[pltpu.VMEM((t, d), BF16), pltpu.VMEM((t + CONV_HALO, ch), F32)],
        compiler_params=_params(("arbitrary",)),
        name="mix_in",
    )(x, w, conv_w, conv_b, ln_g, ln_b)


def _split_bf16(x):
    hi = x.astype(BF16)
    lo = (x - hi.astype(F32)).astype(BF16)
    return hi, lo


def _mix_tail_kernel(qkvg_ref, alr_ref, wg_hi_ref, wg_lo_ref, bgate_ref, ng_ref, x_ref, cv_ref,
                     wa_ref, wb_ref, lg_ref, lb_ref, next_w_in_ref, o_ref, next_w_gu_ref,
                     state_ref, gla_ref, *, c):
    @pl.when(pl.program_id(1) == 0)
    def _():
        state_ref[...] = jnp.zeros_like(state_ref)

    n_ff, _, two_tf = next_w_gu_ref.shape
    tf = two_tf // 2
    for j in range(n_ff):
        next_w_gu_ref[j, :, pl.ds(0, tf)] = next_w_in_ref[:, pl.ds(j * tf, tf)].astype(BF16)
        next_w_gu_ref[j, :, pl.ds(tf, tf)] = next_w_in_ref[:, pl.ds((n_ff + j) * tf, tf)].astype(BF16)

    t = qkvg_ref.shape[1]
    row = lax.broadcasted_iota(jnp.int32, (c, c), 0)
    col = lax.broadcasted_iota(jnp.int32, (c, c), 1)
    causal = row >= col
    tril = causal.astype(BF16)
    scale = GLA_DK ** -0.5

    for blk in range(t // c):
        rows = pl.ds(blk * c, c)
        a_hi, a_lo = _split_bf16(alr_ref[0, rows, :])
        z = (_dot(a_hi, wg_hi_ref[...]) + _dot(a_lo, wg_hi_ref[...]) + _dot(a_hi, wg_lo_ref[...])
             + bgate_ref[...])
        log_a = jax.nn.log_sigmoid(z) / GATE_TAU
        la_hi, la_lo = _split_bf16(log_a)
        cum = _dot(tril, la_hi) + _dot(tril, la_lo)
        c_mid = cum[c // 2 - 1:c // 2, :]
        c_last = cum[c - 1:c, :]
        e_q = jnp.exp(cum - c_mid)
        e_k = jnp.exp(c_mid - cum)
        e_in = jnp.exp(cum)
        e_out = jnp.exp(c_last - cum)
        e_last = jnp.exp(c_last)

        for h in range(GLA_HEADS):
            ks = slice(h * GLA_DK, (h + 1) * GLA_DK)
            q = qkvg_ref[0, rows, pl.ds(h * GLA_DK, GLA_DK)].astype(F32) * scale
            k = qkvg_ref[0, rows, pl.ds(GLA_QK + h * GLA_DK, GLA_DK)].astype(F32)
            v = qkvg_ref[0, rows, pl.ds(2 * GLA_QK + h * GLA_DV, GLA_DV)]
            g = qkvg_ref[0, rows, pl.ds(2 * GLA_QK + GLA_V + h * GLA_DV, GLA_DV)].astype(F32)

            q_intra = (q * e_q[:, ks]).astype(BF16)
            k_intra = (k * e_k[:, ks]).astype(BF16)
            q_inter = (q * e_in[:, ks]).astype(BF16)
            k_out = (k * e_out[:, ks]).astype(BF16)

            scores = jnp.where(causal, _dot_nt(q_intra, k_intra), 0.0).astype(BF16)
            state_t = state_ref[h]
            o = _dot(scores, v) + _dot_nt(q_inter, state_t.astype(BF16))
            state_ref[h] = state_t * e_last[:, ks] + _dot_tn(v, k_out)

            o = o * lax.rsqrt(jnp.mean(o * o, axis=-1, keepdims=True) + LN_EPS) * ng_ref[...]
            gla_ref[rows, pl.ds(h * GLA_DV, GLA_DV)] = (o * _silu(g)).astype(gla_ref.dtype)

    for r0 in range(0, t, EPILOGUE_ROWS):
        rows = pl.ds(r0, EPILOGUE_ROWS)
        mix = _dot(gla_ref[rows, :], wa_ref[...]) + _dot(cv_ref[0, rows, :], wb_ref[...])
        o_ref[0, rows, :] = _layer_norm(ALPHA * x_ref[0, rows, :] + mix, lg_ref[...], lb_ref[...])


def _mix_tail(qkvg, alr, w_gate, b_gate, norm_g, x, cv, w_out, ln_g, ln_b, next_w_in, *, t, c, tf):
    bsz, s, d = x.shape
    assert s % t == 0 and t % c == 0 and t % EPILOGUE_ROWS == 0
    assert w_out.shape == (GLA_V + cv.shape[2], d) and cv.shape[2] == GLA_V
    wg_hi, wg_lo = _split_bf16(w_gate)
    tile = lambda b, i: (b, i, 0)
    const = lambda b, i: (0, 0)
    n_steps = bsz * (s // t)
    dw, two_ff = next_w_in.shape
    n_ff = two_ff // 2 // tf
    slab = dw // n_steps
    assert dw % n_steps == 0 and slab % (2 * SUBLANES) == 0 and two_ff == 2 * n_ff * tf
    return pl.pallas_call(
        functools.partial(_mix_tail_kernel, c=c),
        grid=(bsz, s // t),
        in_specs=[
            pl.BlockSpec((1, t, QKVG_COLS), tile),
            pl.BlockSpec((1, t, GATE_RANK), tile),
            pl.BlockSpec((GATE_RANK, GLA_QK), const),
            pl.BlockSpec((GATE_RANK, GLA_QK), const),
            pl.BlockSpec((1, GLA_QK), const),
            pl.BlockSpec((1, GLA_DV), const),
            pl.BlockSpec((1, t, d), tile),
            pl.BlockSpec((1, t, GLA_V), tile),
            pl.BlockSpec((GLA_V, d), const, pipeline_mode=pl.Buffered(1)),
            pl.BlockSpec((GLA_V, d), lambda b, i: (1, 0), pipeline_mode=pl.Buffered(1)),
            pl.BlockSpec((1, d), const),
            pl.BlockSpec((1, d), const),
            pl.BlockSpec((slab, two_ff), lambda b, i: (b * (s // t) + i, 0)),
        ],
        out_specs=[
            pl.BlockSpec((1, t, d), tile),
            pl.BlockSpec((n_ff, slab, 2 * tf), lambda b, i: (0, b * (s // t) + i, 0)),
        ],
        out_shape=[
            jax.ShapeDtypeStruct((bsz, s, d), F32),
            jax.ShapeDtypeStruct((n_ff, dw, 2 * tf), BF16),
        ],
        scratch_shapes=[pltpu.VMEM((GLA_HEADS, GLA_DV, GLA_DK), F32), pltpu.VMEM((t, GLA_V), BF16)],
        compiler_params=_params(("parallel", "arbitrary")),
        name="mix_tail",
    )(qkvg, alr, wg_hi, wg_lo, b_gate, norm_g, x, cv, w_out, w_out, ln_g, ln_b, next_w_in)


def _ple_kernel(x_ref, p_ref, wg_ref, bg_ref, wp_ref, g_ref, b_ref, o_ref):
    for r0 in range(0, x_ref.shape[0], EPILOGUE_ROWS):
        rows = pl.ds(r0, EPILOGUE_ROWS)
        x = x_ref[rows, :]
        gate = jax.nn.sigmoid(_dot(x.astype(BF16), wg_ref[...]) + bg_ref[...])
        emb = _dot(p_ref[rows, :].astype(BF16), wp_ref[...])
        o_ref[rows, :] = _layer_norm(ALPHA * x + gate * emb, g_ref[...], b_ref[...])


def _ple(x, p, w_gate, b_gate, w_proj, g, b, *, tm):
    m, d = x.shape
    dp = p.shape[1]
    assert m % tm == 0
    return pl.pallas_call(
        _ple_kernel,
        grid=(m // tm,),
        in_specs=[
            pl.BlockSpec((tm, d), lambda i: (i, 0)),
            pl.BlockSpec((tm, dp), lambda i: (i, 0)),
            pl.BlockSpec((d, d), lambda i: (0, 0), pipeline_mode=pl.Buffered(1)),
            pl.BlockSpec((1, d), lambda i: (0, 0)),
            pl.BlockSpec((dp, d), lambda i: (0, 0), pipeline_mode=pl.Buffered(1)),
            pl.BlockSpec((1, d), lambda i: (0, 0)),
            pl.BlockSpec((1, d), lambda i: (0, 0)),
        ],
        out_specs=pl.BlockSpec((tm, d), lambda i: (i, 0)),
        out_shape=jax.ShapeDtypeStruct((m, d), F32),
        compiler_params=_params(("parallel",)),
        name="ple",
    )(x, p, w_gate, b_gate, w_proj, g, b)


def _layer(x, p, ffn1_w_in, ffn1_w_out, ln_ffn1_g, ln_ffn1_b, w_mix_in, gla_w_gate,
           gla_b_gate, gla_norm_g, conv_w, conv_b, conv_ln_g, conv_ln_b, w_mix_out,
           ln_mix_g, ln_mix_b, ffn2_w_in, ffn2_w_out, ln_ffn2_g, ln_ffn2_b,
           ple_w_gate, ple_b_gate, ple_w_proj, ln_ple_g, ln_ple_b, *, tiles):
    bsz, s, d = x.shape
    m = bsz * s
    row = lambda a: a.reshape(1, -1)
    x2 = x.reshape(m, d)

    x2 = _ffn(x2, _pack_w_in(ffn1_w_in, tiles["ffn_tf"]), ffn1_w_out, row(ln_ffn1_g), row(ln_ffn1_b),
              tm=tiles["ffn_tm"])

    w_in = jnp.pad(w_mix_in.astype(BF16), ((0, 0), (0, -w_mix_in.shape[1] % LANES)))
    assert s % tiles["mix_t"] == 0
    qkvg, alr, cv = _mix_in(x2, w_in, conv_w, row(conv_b), row(conv_ln_g), row(conv_ln_b),
                            t=tiles["mix_t"], tiles_per_seq=s // tiles["mix_t"])
    x3, ffn2_w_gu = _mix_tail(qkvg.reshape(bsz, s, -1), alr.reshape(bsz, s, -1), gla_w_gate,
                              row(gla_b_gate), row(gla_norm_g), x2.reshape(bsz, s, d),
                              cv.reshape(bsz, s, -1), w_mix_out.astype(BF16), row(ln_mix_g),
                              row(ln_mix_b), ffn2_w_in,
                              t=tiles["gla_t"], c=tiles["gla_c"], tf=tiles["ffn_tf"])

    x2 = _ffn(x3.reshape(m, d), ffn2_w_gu, ffn2_w_out, row(ln_ffn2_g), row(ln_ffn2_b),
              tm=tiles["ffn_tm"])

    x2 = _ple(x2, p.reshape(m, -1), ple_w_gate.astype(BF16), row(ple_b_gate),
              ple_w_proj.astype(BF16), row(ln_ple_g), row(ln_ple_b), tm=tiles["out_tm"])
    return x2.reshape(bsz, s, d)


TILES = dict(ffn_tm=1024, ffn_tf=512, mix_t=512, gla_t=512, gla_c=256, out_tm=1024)


def kernel(x, p, ffn1_w_in, ffn1_w_out, ln_ffn1_g, ln_ffn1_b, w_mix_in, gla_w_gate, gla_b_gate, gla_norm_g, conv_w, conv_b, conv_ln_g, conv_ln_b, w_mix_out, ln_mix_g, ln_mix_b, ffn2_w_in, ffn2_w_out, ln_ffn2_g, ln_ffn2_b, ple_w_gate, ple_b_gate, ple_w_proj, ln_ple_g, ln_ple_b):
    assert x.shape[0] == p.shape[1] and ffn1_w_in.shape[0] == DEPTH
    return _layer(x, p[0], ffn1_w_in[0], ffn1_w_out[0], ln_ffn1_g[0], ln_ffn1_b[0], w_mix_in[0],
                  gla_w_gate[0], gla_b_gate[0], gla_norm_g[0], conv_w[0], conv_b[0], conv_ln_g[0],
                  conv_ln_b[0], w_mix_out[0], ln_mix_g[0], ln_mix_b[0], ffn2_w_in[0], ffn2_w_out[0],
                  ln_ffn2_g[0], ln_ffn2_b[0], ple_w_gate[0], ple_b_gate[0], ple_w_proj[0],
                  ln_ple_g[0], ln_ple_b[0], tiles=TILES)
```

```python
import functools

import jax
import jax.numpy as jnp
from jax import lax
from jax.experimental import pallas as pl
from jax.experimental.pallas import tpu as pltpu

DEPTH = 1
GLA_HEADS = 4
GLA_DK = 128
GLA_DV = 256
GLA_QK = GLA_HEADS * GLA_DK
GLA_V = GLA_HEADS * GLA_DV
CONV_CH = 1024
GATE_RANK = 16
GATE_TAU = 16.0
CONV_WIDTH = 31
LN_EPS = 1e-5
ALPHA = (2.0 * DEPTH) ** 0.25

QKVG_COLS = 2 * GLA_QK + 2 * GLA_V
GLU_START = QKVG_COLS + GATE_RANK

SUBLANES = 8
LANES = 128
CONV_HALO = 32
EPILOGUE_ROWS = 256

BF16 = jnp.bfloat16
F32 = jnp.float32

VMEM_LIMIT = 56 * 1024 * 1024
FFN_VMEM_LIMIT = 62 * 1024 * 1024


def _dot(a, b):
    return jnp.dot(a, b, preferred_element_type=F32)


def _dot_nt(a, b):
    return lax.dot_general(a, b, (((1,), (1,)), ((), ())), preferred_element_type=F32)


def _dot_tn(a, b):
    return lax.dot_general(a, b, (((0,), (0,)), ((), ())), preferred_element_type=F32)


def _layer_norm(y, g, b):
    mu = jnp.mean(y, axis=-1, keepdims=True)
    d = y - mu
    var = jnp.mean(d * d, axis=-1, keepdims=True)
    return d * lax.rsqrt(var + LN_EPS) * g + b


def _silu(x):
    return x * jax.nn.sigmoid(x)


def _params(sem, vmem_limit=VMEM_LIMIT):
    return pltpu.CompilerParams(dimension_semantics=sem, vmem_limit_bytes=vmem_limit)


def _ffn_kernel(x_ref, wgu_ref, wo_ref, g_ref, b_ref, o_ref, xb_ref, act_ref, *, n_ff):
    s = pl.program_id(0)
    n_steps = pl.num_programs(0) - 1
    j_prev = (s + n_ff - 1) % n_ff

    def stage_a(cast_x):
        if cast_x:
            xb_ref[...] = x_ref[...].astype(BF16)
        tf = act_ref.shape[2]
        gate_up = _dot(xb_ref[...], wgu_ref[0])
        act_ref[s % 2] = (0.5 * _silu(gate_up[:, :tf]) * gate_up[:, tf:]).astype(BF16)

    def stage_b(first, last):
        chunk = EPILOGUE_ROWS if last else o_ref.shape[0]
        wo = wo_ref[...].astype(BF16)
        for r0 in range(0, o_ref.shape[0], chunk):
            rows = pl.ds(r0, chunk)
            part = _dot(act_ref[(s + 1) % 2, rows, :], wo)
            if first:
                acc = ALPHA * x_ref[rows, :] + part
            else:
                acc = o_ref[rows, :] + part
            if last:
                acc = _layer_norm(acc, g_ref[...], b_ref[...])
            o_ref[rows, :] = acc

    first_b = j_prev == 0
    last_b = j_prev == n_ff - 1

    @pl.when(s == 0)
    def _():
        stage_a(True)

    @pl.when(jnp.logical_and(s > 0, first_b))
    def _():
        stage_b(True, False)
        stage_a(False)

    @pl.when(jnp.logical_and(jnp.logical_not(first_b), jnp.logical_not(last_b)))
    def _():
        stage_b(False, False)
        stage_a(False)

    @pl.when(jnp.logical_and(last_b, jnp.logical_and(s > 0, s < n_steps)))
    def _():
        stage_b(False, True)
        stage_a(True)

    @pl.when(s == n_steps)
    def _():
        stage_b(False, True)


def _pack_kernel(gate_ref, up_ref, o_ref):
    tf = gate_ref.shape[1]
    o_ref[0, :, pl.ds(0, tf)] = gate_ref[...].astype(BF16)
    o_ref[0, :, pl.ds(tf, tf)] = up_ref[...].astype(BF16)


def _pack_w_in(w_in, tf):
    d, two_ff = w_in.shape
    d_ff = two_ff // 2
    n_ff = d_ff // tf
    assert d_ff % tf == 0
    return pl.pallas_call(
        _pack_kernel,
        grid=(n_ff,),
        in_specs=[
            pl.BlockSpec((d, tf), lambda j: (0, j)),
            pl.BlockSpec((d, tf), lambda j: (0, j + n_ff)),
        ],
        out_specs=pl.BlockSpec((1, d, 2 * tf), lambda j: (j, 0, 0)),
        out_shape=jax.ShapeDtypeStruct((n_ff, d, 2 * tf), BF16),
        compiler_params=_params(("parallel",)),
        name="pack_w_in",
    )(w_in, w_in)


def _ffn(x, w_gu, w_out, g, b, *, tm):
    m, d = x.shape
    n_ff, _, two_tf = w_gu.shape
    tf = two_tf // 2
    assert m % tm == 0 and w_out.shape[0] == n_ff * tf and n_ff >= 2
    n_steps = (m // tm) * n_ff
    cur = lambda s: jnp.minimum(s, n_steps - 1)
    prev = lambda s: jnp.maximum(s - 1, 0)
    return pl.pallas_call(
        functools.partial(_ffn_kernel, n_ff=n_ff),
        grid=(n_steps + 1,),
        in_specs=[
            pl.BlockSpec((tm, d), lambda s: (cur(s) // n_ff, 0)),
            pl.BlockSpec((1, d, two_tf), lambda s: (cur(s) % n_ff, 0, 0)),
            pl.BlockSpec((tf, d), lambda s: (prev(s) % n_ff, 0)),
            pl.BlockSpec((1, d), lambda s: (0, 0)),
            pl.BlockSpec((1, d), lambda s: (0, 0)),
        ],
        out_specs=pl.BlockSpec((tm, d), lambda s: (prev(s) // n_ff, 0)),
        out_shape=jax.ShapeDtypeStruct((m, d), F32),
        scratch_shapes=[pltpu.VMEM((tm, d), BF16), pltpu.VMEM((2, tm, tf), BF16)],
        compiler_params=_params(("arbitrary",), FFN_VMEM_LIMIT),
        name="ffn",
    )(x, w_gu, w_out, g, b)


CONV_RB = 64
PROJ_NB = 768
GLU_ROWS = 256


def _mix_in_kernel(x_ref, w_ref, cw_ref, cb_ref, lg_ref, lb_ref, late_a_ref, late_b_ref,
                   qkvg_ref, alr_ref, cv_ref, late_a_bf_ref, late_b_bf_ref,
                   xb_ref, win_ref, *, tiles_per_seq):
    i = pl.program_id(0)
    t = x_ref.shape[0]
    ch = cv_ref.shape[1]

    late_a_bf_ref[...] = late_a_ref[...].astype(BF16)
    late_b_bf_ref[...] = late_b_ref[...].astype(BF16)

    @pl.when(i == 0)
    def _():
        win_ref[pl.ds(t, CONV_HALO), :] = jnp.zeros((CONV_HALO, ch), F32)

    tail = win_ref[pl.ds(t, CONV_HALO), :]
    win_ref[pl.ds(0, CONV_HALO), :] = jnp.where(i % tiles_per_seq == 0, 0.0, tail)

    xb_ref[...] = x_ref[...].astype(BF16)
    for c0 in range(0, QKVG_COLS, PROJ_NB):
        cols = pl.ds(c0, PROJ_NB)
        qkvg_ref[:, cols] = _dot(xb_ref[...], w_ref[:, cols]).astype(BF16)
    n_rest = w_ref.shape[1] - QKVG_COLS
    for r0 in range(0, t, GLU_ROWS):
        rows = pl.ds(r0, GLU_ROWS)
        rest = _dot(xb_ref[rows, :], w_ref[:, pl.ds(QKVG_COLS, n_rest)])
        alr_ref[rows, :] = rest[:, :GATE_RANK]
        a = rest[:, GATE_RANK:GATE_RANK + ch]
        gate = rest[:, GATE_RANK + ch:GATE_RANK + 2 * ch]
        win_ref[pl.ds(CONV_HALO + r0, GLU_ROWS), :] = a * jax.nn.sigmoid(gate)

    first_off = CONV_HALO - (CONV_WIDTH - 1)
    ext = CONV_RB + CONV_HALO
    for r0 in range(0, t, CONV_RB):
        blocks = []
        for l0 in range(0, ch, LANES):
            lanes = pl.ds(l0, LANES)
            w = win_ref[pl.ds(r0, ext), lanes]
            acc = None
            for r in range(SUBLANES):
                wr = w if r == 0 else pltpu.roll(w, ext - r, 0)
                for off in range(first_off, first_off + CONV_WIDTH):
                    if off % SUBLANES == r:
                        tap = wr[off - r:off - r + CONV_RB] * cw_ref[pl.ds(off - first_off, 1), lanes]
                        acc = tap if acc is None else acc + tap
            blocks.append(acc)
        y = jnp.concatenate(blocks, axis=1) + cb_ref[...]
        y = _layer_norm(y, lg_ref[...], lb_ref[...])
        cv_ref[pl.ds(r0, CONV_RB), :] = _silu(y).astype(cv_ref.dtype)


def _mix_in(x, w, conv_w, conv_b, ln_g, ln_b, late_a, late_b, *, t, tiles_per_seq):
    m, d = x.shape
    ch = conv_w.shape[1]
    assert m % t == 0 and t % CONV_RB == 0 and ch % LANES == 0
    assert QKVG_COLS % PROJ_NB == 0 and t % GLU_ROWS == 0
    assert w.shape[1] % LANES == 0 and w.shape[1] >= GLU_START + 2 * ch
    n_steps = m // t
    slab_a, slab_b = late_a.shape[0] // n_steps, late_b.shape[0] // n_steps
    assert late_a.shape[0] % n_steps == 0 and late_b.shape[0] % n_steps == 0
    assert slab_a % (2 * SUBLANES) == 0 and slab_b % (2 * SUBLANES) == 0
    tile = lambda i: (i, 0)
    const = lambda i: (0, 0)
    return pl.pallas_call(
        functools.partial(_mix_in_kernel, tiles_per_seq=tiles_per_seq),
        grid=(m // t,),
        in_specs=[
            pl.BlockSpec((t, d), tile),
            pl.BlockSpec(w.shape, const, pipeline_mode=pl.Buffered(1)),
            pl.BlockSpec((CONV_WIDTH, ch), const),
            pl.BlockSpec((1, ch), const),
            pl.BlockSpec((1, ch), const),
            pl.BlockSpec((1, ch), const),
            pl.BlockSpec((slab_a, late_a.shape[1]), tile),
            pl.BlockSpec((slab_b, late_b.shape[1]), tile),
        ],
        out_specs=[
            pl.BlockSpec((t, QKVG_COLS), tile),
            pl.BlockSpec((t, GATE_RANK), tile),
            pl.BlockSpec((t, ch), tile),
            pl.BlockSpec((slab_a, late_a.shape[1]), tile),
            pl.BlockSpec((slab_b, late_b.shape[1]), tile),
        ],
        out_shape=[
            jax.ShapeDtypeStruct((m, QKVG_COLS), BF16),
            jax.ShapeDtypeStruct((m, GATE_RANK), F32),
            jax.ShapeDtypeStruct((m, ch), BF16),
            jax.ShapeDtypeStruct(late_a.shape, BF16),
            jax.ShapeDtypeStruct(late_b.shape, BF16),
        ],
        scratch_shapes=[pltpu.VMEM((t, d), BF16), pltpu.VMEM((t + CONV_HALO, ch), F32)],
        compiler_params=_params(("arbitrary",)),
        name="mix_in",
    )(x, w, conv_w, conv_b, ln_g, ln_b, late_a, late_b)


def _split_bf16(x):
    hi = x.astype(BF16)
    lo = (x - hi.astype(F32)).astype(BF16)
    return hi, lo


def _mix_tail_kernel(qkvg_ref, alr_ref, wg_hi_ref, wg_lo_ref, bgate_ref, ng_ref, x_ref, cv_ref,
                     wa_ref, wb_ref, lg_ref, lb_ref, next_w_in_ref, o_ref, next_w_gu_ref,
                     state_ref, gla_ref, *, c):
    @pl.when(pl.program_id(1) == 0)
    def _():
        state_ref[...] = jnp.zeros_like(state_ref)

    n_ff, _, two_tf = next_w_gu_ref.shape
    tf = two_tf // 2
    for j in range(n_ff):
        next_w_gu_ref[j, :, pl.ds(0, tf)] = next_w_in_ref[:, pl.ds(j * tf, tf)].astype(BF16)
        next_w_gu_ref[j, :, pl.ds(tf, tf)] = next_w_in_ref[:, pl.ds((n_ff + j) * tf, tf)].astype(BF16)

    t = qkvg_ref.shape[1]
    row = lax.broadcasted_iota(jnp.int32, (c, c), 0)
    col = lax.broadcasted_iota(jnp.int32, (c, c), 1)
    causal = row >= col
    tril = causal.astype(BF16)
    scale = GLA_DK ** -0.5

    for blk in range(t // c):
        rows = pl.ds(blk * c, c)
        a_hi, a_lo = _split_bf16(alr_ref[0, rows, :])
        z = (_dot(a_hi, wg_hi_ref[...]) + _dot(a_lo, wg_hi_ref[...]) + _dot(a_hi, wg_lo_ref[...])
             + bgate_ref[...])
        log_a = jax.nn.log_sigmoid(z) / GATE_TAU
        la_hi, la_lo = _split_bf16(log_a)
        cum = _dot(tril, la_hi) + _dot(tril, la_lo)
        c_mid = cum[c // 2 - 1:c // 2, :]
        c_last = cum[c - 1:c, :]
        e_q = jnp.exp(cum - c_mid)
        e_k = jnp.exp(c_mid - cum)
        e_in = jnp.exp(cum)
        e_out = jnp.exp(c_last - cum)
        e_last = jnp.exp(c_last)

        for h in range(GLA_HEADS):
            ks = slice(h * GLA_DK, (h + 1) * GLA_DK)
            q = qkvg_ref[0, rows, pl.ds(h * GLA_DK, GLA_DK)].astype(F32) * scale
            k = qkvg_ref[0, rows, pl.ds(GLA_QK + h * GLA_DK, GLA_DK)].astype(F32)
            v = qkvg_ref[0, rows, pl.ds(2 * GLA_QK + h * GLA_DV, GLA_DV)]
            g = qkvg_ref[0, rows, pl.ds(2 * GLA_QK + GLA_V + h * GLA_DV, GLA_DV)].astype(F32)

            q_intra = (q * e_q[:, ks]).astype(BF16)
            k_intra = (k * e_k[:, ks]).astype(BF16)
            q_inter = (q * e_in[:, ks]).astype(BF16)
            k_out = (k * e_out[:, ks]).astype(BF16)

            scores = jnp.where(causal, _dot_nt(q_intra, k_intra), 0.0).astype(BF16)
            state_t = state_ref[h]
            o = _dot(scores, v) + _dot_nt(q_inter, state_t.astype(BF16))
            state_ref[h] = state_t * e_last[:, ks] + _dot_tn(v, k_out)

            o = o * lax.rsqrt(jnp.mean(o * o, axis=-1, keepdims=True) + LN_EPS) * ng_ref[...]
            gla_ref[rows, pl.ds(h * GLA_DV, GLA_DV)] = (o * _silu(g)).astype(gla_ref.dtype)

    for r0 in range(0, t, EPILOGUE_ROWS):
        rows = pl.ds(r0, EPILOGUE_ROWS)
        mix = _dot(gla_ref[rows, :], wa_ref[...]) + _dot(cv_ref[0, rows, :], wb_ref[...])
        o_ref[0, rows, :] = _layer_norm(ALPHA * x_ref[0, rows, :] + mix, lg_ref[...], lb_ref[...])


def _mix_tail(qkvg, alr, w_gate, b_gate, norm_g, x, cv, w_out, ln_g, ln_b, next_w_in, *, t, c, tf):
    bsz, s, d = x.shape
    assert s % t == 0 and t % c == 0 and t % EPILOGUE_ROWS == 0
    assert w_out.shape == (GLA_V + cv.shape[2], d) and cv.shape[2] == GLA_V
    wg_hi, wg_lo = _split_bf16(w_gate)
    tile = lambda b, i: (b, i, 0)
    const = lambda b, i: (0, 0)
    n_steps = bsz * (s // t)
    dw, two_ff = next_w_in.shape
    n_ff = two_ff // 2 // tf
    slab = dw // n_steps
    assert dw % n_steps == 0 and slab % (2 * SUBLANES) == 0 and two_ff == 2 * n_ff * tf
    return pl.pallas_call(
        functools.partial(_mix_tail_kernel, c=c),
        grid=(bsz, s // t),
        in_specs=[
            pl.BlockSpec((1, t, QKVG_COLS), tile),
            pl.BlockSpec((1, t, GATE_RANK), tile),
            pl.BlockSpec((GATE_RANK, GLA_QK), const),
            pl.BlockSpec((GATE_RANK, GLA_QK), const),
            pl.BlockSpec((1, GLA_QK), const),
            pl.BlockSpec((1, GLA_DV), const),
            pl.BlockSpec((1, t, d), tile),
            pl.BlockSpec((1, t, GLA_V), tile),
            pl.BlockSpec((GLA_V, d), const, pipeline_mode=pl.Buffered(1)),
            pl.BlockSpec((GLA_V, d), lambda b, i: (1, 0), pipeline_mode=pl.Buffered(1)),
            pl.BlockSpec((1, d), const),
            pl.BlockSpec((1, d), const),
            pl.BlockSpec((slab, two_ff), lambda b, i: (b * (s // t) + i, 0)),
        ],
        out_specs=[
            pl.BlockSpec((1, t, d), tile),
            pl.BlockSpec((n_ff, slab, 2 * tf), lambda b, i: (0, b * (s // t) + i, 0)),
        ],
        out_shape=[
            jax.ShapeDtypeStruct((bsz, s, d), F32),
            jax.ShapeDtypeStruct((n_ff, dw, 2 * tf), BF16),
        ],
        scratch_shapes=[pltpu.VMEM((GLA_HEADS, GLA_DV, GLA_DK), F32), pltpu.VMEM((t, GLA_V), BF16)],
        compiler_params=_params(("parallel", "arbitrary")),
        name="mix_tail",
    )(qkvg, alr, wg_hi, wg_lo, b_gate, norm_g, x, cv, w_out, w_out, ln_g, ln_b, next_w_in)


def _ple_kernel(x_ref, p_ref, wg_ref, bg_ref, wp_ref, g_ref, b_ref, o_ref):
    for r0 in range(0, x_ref.shape[0], EPILOGUE_ROWS):
        rows = pl.ds(r0, EPILOGUE_ROWS)
        x = x_ref[rows, :]
        gate = jax.nn.sigmoid(_dot(x.astype(BF16), wg_ref[...]) + bg_ref[...])
        emb = _dot(p_ref[rows, :].astype(BF16), wp_ref[...])
        o_ref[rows, :] = _layer_norm(ALPHA * x + gate * emb, g_ref[...], b_ref[...])


def _ple(x, p, w_gate, b_gate, w_proj, g, b, *, tm):
    m, d = x.shape
    dp = p.shape[1]
    assert m % tm == 0
    return pl.pallas_call(
        _ple_kernel,
        grid=(m // tm,),
        in_specs=[
            pl.BlockSpec((tm, d), lambda i: (i, 0)),
            pl.BlockSpec((tm, dp), lambda i: (i, 0)),
            pl.BlockSpec((d, d), lambda i: (0, 0), pipeline_mode=pl.Buffered(1)),
            pl.BlockSpec((1, d), lambda i: (0, 0)),
            pl.BlockSpec((dp, d), lambda i: (0, 0), pipeline_mode=pl.Buffered(1)),
            pl.BlockSpec((1, d), lambda i: (0, 0)),
            pl.BlockSpec((1, d), lambda i: (0, 0)),
        ],
        out_specs=pl.BlockSpec((tm, d), lambda i: (i, 0)),
        out_shape=jax.ShapeDtypeStruct((m, d), F32),
        compiler_params=_params(("parallel",)),
        name="ple",
    )(x, p, w_gate, b_gate, w_proj, g, b)


def _layer(x, p, ffn1_w_in, ffn1_w_out, ln_ffn1_g, ln_ffn1_b, w_mix_in, gla_w_gate,
           gla_b_gate, gla_norm_g, conv_w, conv_b, conv_ln_g, conv_ln_b, w_mix_out,
           ln_mix_g, ln_mix_b, ffn2_w_in, ffn2_w_out, ln_ffn2_g, ln_ffn2_b,
           ple_w_gate, ple_b_gate, ple_w_proj, ln_ple_g, ln_ple_b, *, tiles):
    bsz, s, d = x.shape
    m = bsz * s
    row = lambda a: a.reshape(1, -1)
    x2 = x.reshape(m, d)

    x2 = _ffn(x2, _pack_w_in(ffn1_w_in, tiles["ffn_tf"]), ffn1_w_out, row(ln_ffn1_g), row(ln_ffn1_b),
              tm=tiles["ffn_tm"])

    w_in = jnp.pad(w_mix_in.astype(BF16), ((0, 0), (0, -w_mix_in.shape[1] % LANES)))
    assert s % tiles["mix_t"] == 0
    qkvg, alr, cv, w_mix_out_bf, ple_w_gate_bf = _mix_in(
        x2, w_in, conv_w, row(conv_b), row(conv_ln_g), row(conv_ln_b), w_mix_out, ple_w_gate,
        t=tiles["mix_t"], tiles_per_seq=s // tiles["mix_t"])
    x3, ffn2_w_gu = _mix_tail(qkvg.reshape(bsz, s, -1), alr.reshape(bsz, s, -1), gla_w_gate,
                              row(gla_b_gate), row(gla_norm_g), x2.reshape(bsz, s, d),
                              cv.reshape(bsz, s, -1), w_mix_out_bf, row(ln_mix_g),
                              row(ln_mix_b), ffn2_w_in,
                              t=tiles["gla_t"], c=tiles["gla_c"], tf=tiles["ffn_tf"])

    x2 = _ffn(x3.reshape(m, d), ffn2_w_gu, ffn2_w_out, row(ln_ffn2_g), row(ln_ffn2_b),
              tm=tiles["ffn_tm"])

    x2 = _ple(x2, p.reshape(m, -1), ple_w_gate_bf, row(ple_b_gate),
              ple_w_proj.astype(BF16), row(ln_ple_g), row(ln_ple_b), tm=tiles["out_tm"])
    return x2.reshape(bsz, s, d)


TILES = dict(ffn_tm=1024, ffn_tf=512, mix_t=512, gla_t=512, gla_c=256, out_tm=1024)


def kernel(x, p, ffn1_w_in, ffn1_w_out, ln_ffn1_g, ln_ffn1_b, w_mix_in, gla_w_gate, gla_b_gate, gla_norm_g, conv_w, conv_b, conv_ln_g, conv_ln_b, w_mix_out, ln_mix_g, ln_mix_b, ffn2_w_in, ffn2_w_out, ln_ffn2_g, ln_ffn2_b, ple_w_gate, ple_b_gate, ple_w_proj, ln_ple_g, ln_ple_b):
    assert x.shape[0] == p.shape[1] and ffn1_w_in.shape[0] == DEPTH
    return _layer(x, p[0], ffn1_w_in[0], ffn1_w_out[0], ln_ffn1_g[0], ln_ffn1_b[0], w_mix_in[0],
                  gla_w_gate[0], gla_b_gate[0], gla_norm_g[0], conv_w[0], conv_b[0], conv_ln_g[0],
                  conv_ln_b[0], w_mix_out[0], ln_mix_g[0], ln_mix_b[0], ffn2_w_in[0], ffn2_w_out[0],
                  ln_ffn2_g[0], ln_ffn2_b[0], ple_w_gate[0], ple_b_gate[0], ple_w_proj[0],
                  ln_ple_g[0], ln_ple_b[0], tiles=TILES)
```
